```python
import jax, jax.numpy as jnp
from jax import lax
import numpy as np

D_MODEL = 1024
BATCH = 8
SEQ = 4096
DEPTH = 1

HEAD_DIM = 64
RWKV_HEADS = 8
FOX_HEADS = 8
RWKV_DIM = RWKV_HEADS * HEAD_DIM
FOX_DIM = FOX_HEADS * HEAD_DIM
MIX_DIM = RWKV_DIM + FOX_DIM
DECAY_LORA = 64
AAA_LORA = 64
GATE_LORA = 128
RWKV_COLS = 3 * RWKV_DIM + DECAY_LORA + AAA_LORA + GATE_LORA
FOX_COLS = 4 * FOX_DIM + FOX_HEADS
IN_COLS = RWKV_COLS + FOX_COLS
D_FF = 2816
CONV_WIDTH = 3
Q_BLOCK = 128
NORM_EPS = 1e-6
LNX_EPS = 64e-5

kernel_name = "hymba_rwkv7_fox_convffn"


def rmsnorm(x, w, eps=NORM_EPS):
    x32 = x.astype(jnp.float32)
    y = x32 * lax.rsqrt(jnp.mean(x32 * x32, axis=-1, keepdims=True) + eps)
    return y * w.astype(jnp.float32)


def wkv7_scan(r, w, k, v, a, b):
    B, T, H, N = r.shape
    xs = tuple(jnp.moveaxis(t.astype(jnp.float32), 1, 0) for t in (r, w, k, v, a, b))

    def step(S, inp):
        r_t, w_t, k_t, v_t, a_t, b_t = inp
        sa = jnp.einsum('bhvk,bhk->bhv', S, a_t)
        S = S * w_t[:, :, None, :] + sa[..., None] * b_t[:, :, None, :] + v_t[..., None] * k_t[:, :, None, :]
        y = jnp.einsum('bhvk,bhk->bhv', S, r_t)
        return S, y

    S0 = jnp.zeros((B, H, N, N), jnp.float32)
    _, ys = lax.scan(step, S0, xs)
    return jnp.moveaxis(ys, 0, 1)


def rwkv7_group(p, mu, w0, w2, a0, a2, g2, k_k, k_a, r_k, lnx_w, lnx_b):
    B, T, _ = p.shape
    p = p.astype(jnp.float32)
    prev = jnp.pad(p, ((0, 0), (1, 0), (0, 0)))[:, :T]
    p = p + (prev - p) * mu.astype(jnp.float32)
    r, k, v, wl, al, gl = jnp.split(
        p, [RWKV_DIM, 2 * RWKV_DIM, 3 * RWKV_DIM, 3 * RWKV_DIM + DECAY_LORA,
            3 * RWKV_DIM + DECAY_LORA + AAA_LORA], axis=-1)
    w = -jax.nn.softplus(-(w0 + jnp.tanh(wl) @ w2)) - 0.5
    decay = jnp.exp(-jnp.exp(w))
    a = jax.nn.sigmoid(a0 + al @ a2)
    g = jax.nn.sigmoid(gl) @ g2
    heads = lambda t: t.reshape(B, T, RWKV_HEADS, HEAD_DIM)
    kk = heads(k * k_k)
    kk = kk * lax.rsqrt(jnp.maximum(jnp.sum(kk * kk, axis=-1, keepdims=True), 1e-24))
    k = k * (1.0 + (a - 1.0) * k_a)
    r_h, k_h, v_h, a_h, w_h = heads(r), heads(k), heads(v), heads(a), heads(decay)
    y = wkv7_scan(r_h, w_h, k_h, v_h, -kk, kk * a_h)
    mean = jnp.mean(y, axis=-1, keepdims=True)
    var = jnp.mean(jnp.square(y - mean), axis=-1, keepdims=True)
    y = ((y - mean) * lax.rsqrt(var + LNX_EPS)).reshape(B, T, RWKV_DIM) * lnx_w + lnx_b
    bonus = jnp.sum(r_h * k_h * r_k, axis=-1, keepdims=True) * v_h
    y = y + bonus.reshape(B, T, RWKV_DIM)
    return y * g


def fox_group(p, f_bias, q_norm_w, k_norm_w, o_norm_w):
    B, T, _ = p.shape
    p = p.astype(jnp.float32)
    q, k, v, og, fl = jnp.split(p, [FOX_DIM, 2 * FOX_DIM, 3 * FOX_DIM, 4 * FOX_DIM], axis=-1)
    heads = lambda t: t.reshape(B, T, FOX_HEADS, HEAD_DIM).transpose(0, 2, 1, 3)
    q = heads(rmsnorm(q.reshape(B, T, FOX_HEADS, HEAD_DIM), q_norm_w).reshape(B, T, FOX_DIM))
    k = heads(rmsnorm(k.reshape(B, T, FOX_HEADS, HEAD_DIM), k_norm_w).reshape(B, T, FOX_DIM))
    v = heads(v)
    log_f = jax.nn.log_sigmoid(fl + f_bias.astype(jnp.float32))
    c = jnp.cumsum(log_f, axis=1).transpose(0, 2, 1)
    n_blocks = T // Q_BLOCK
    qb = q.reshape(B, FOX_HEADS, n_blocks, Q_BLOCK, HEAD_DIM).transpose(2, 0, 1, 3, 4)
    cb = c.reshape(B, FOX_HEADS, n_blocks, Q_BLOCK).transpose(2, 0, 1, 3)
    starts = jnp.arange(n_blocks, dtype=jnp.int32) * Q_BLOCK
    key_pos = jnp.arange(T, dtype=jnp.int32)
    scale = HEAD_DIM ** -0.5

    def block(args):
        q_i, c_i, s0 = args
        s = jnp.einsum('bhqd,bhkd->bhqk', q_i, k) * scale + c_i[..., None] - c[:, :, None, :]
        mask = (s0 + jnp.arange(Q_BLOCK, dtype=jnp.int32))[:, None] >= key_pos[None, :]
        s = jnp.where(mask, s, -jnp.inf)
        return jnp.einsum('bhqk,bhkd->bhqd', jax.nn.softmax(s, axis=-1), v)

    o = lax.map(block, (qb, cb, starts))
    o = o.transpose(1, 0, 3, 2, 4).reshape(B, T, FOX_HEADS, HEAD_DIM)
    o = rmsnorm(o, o_norm_w).reshape(B, T, FOX_DIM)
    return o * jax.nn.sigmoid(og)


def conv_glu_ffn(h, w_up, conv_w, conv_b, w_down):
    u = h @ w_up
    u = lax.conv_general_dilated(
        u, conv_w[:, None, :].astype(u.dtype), window_strides=(1,),
        padding=[(CONV_WIDTH - 1, 0)], dimension_numbers=('NWC', 'WIO', 'NWC'),
        feature_group_count=2 * D_FF) + conv_b
    gate, val = jnp.split(u, 2, axis=-1)
    return (jax.nn.silu(gate) * val) @ w_down


def setup_inputs(seed: int = 0) -> dict:
    key = jax.random.key(seed)
    ks = jax.random.split(key, 26)
    L = DEPTH
    nrm = lambda k, shape, s: jax.random.normal(k, shape, jnp.float32) * s
    uni = lambda k, shape, lo, hi: jax.random.uniform(k, shape, jnp.float32, minval=lo, maxval=hi)
    last_tap = (jnp.arange(CONV_WIDTH) == CONV_WIDTH - 1).astype(jnp.float32)[None, :, None]
    return {
        "x": nrm(ks[0], (BATCH, SEQ, D_MODEL), 1.0),
        "norm_mix_w": 1.0 + nrm(ks[1], (L, D_MODEL), 0.02),
        "w_in": nrm(ks[2], (L, D_MODEL, IN_COLS), D_MODEL ** -0.5),
        "rwkv_mu": uni(ks[3], (L, RWKV_COLS), 0.0, 1.0),
        "rwkv_w0": uni(ks[4], (L, RWKV_DIM), -6.5, -1.5),
        "rwkv_w2": nrm(ks[5], (L, DECAY_LORA, RWKV_DIM), DECAY_LORA ** -0.5),
        "rwkv_a0": nrm(ks[6], (L, RWKV_DIM), 0.1),
        "rwkv_a2": nrm(ks[7], (L, AAA_LORA, RWKV_DIM), AAA_LORA ** -0.5),
        "rwkv_g2": nrm(ks[8], (L, GATE_LORA, RWKV_DIM), GATE_LORA ** -0.5),
        "rwkv_k_k": 0.85 + nrm(ks[9], (L, RWKV_DIM), 0.02),
        "rwkv_k_a": 1.0 + nrm(ks[10], (L, RWKV_DIM), 0.02),
        "rwkv_r_k": nrm(ks[11], (L, RWKV_HEADS, HEAD_DIM), 0.1),
        "rwkv_lnx_w": 1.0 + nrm(ks[12], (L, RWKV_DIM), 0.02),
        "rwkv_lnx_b": nrm(ks[13], (L, RWKV_DIM), 0.02),
        "fox_f_bias": uni(ks[14], (L, FOX_HEADS), 1.0, 6.0),
        "fox_q_norm_w": 1.0 + nrm(ks[15], (L, HEAD_DIM), 0.02),
        "fox_k_norm_w": 1.0 + nrm(ks[16], (L, HEAD_DIM), 0.02),
        "fox_o_norm_w": 1.0 + nrm(ks[17], (L, HEAD_DIM), 0.02),
        "w_out": nrm(ks[18], (L, MIX_DIM, D_MODEL), MIX_DIM ** -0.5),
        "norm_ffn_w": 1.0 + nrm(ks[19], (L, D_MODEL), 0.02),
        "ffn_w_up": nrm(ks[20], (L, D_MODEL, 2 * D_FF), D_MODEL ** -0.5),
        "ffn_conv_w": nrm(ks[21], (L, CONV_WIDTH, 2 * D_FF), 0.2) + last_tap,
        "ffn_conv_b": nrm(ks[22], (L, 2 * D_FF), 0.02),
        "ffn_w_down": nrm(ks[23], (L, D_FF, D_MODEL), D_FF ** -0.5),
        "norm_final_w": 1.0 + nrm(ks[24], (D_MODEL,), 0.02),
    }


def reference(x, norm_mix_w, w_in, rwkv_mu, rwkv_w0, rwkv_w2, rwkv_a0, rwkv_a2, rwkv_g2,
              rwkv_k_k, rwkv_k_a, rwkv_r_k, rwkv_lnx_w, rwkv_lnx_b, fox_f_bias, fox_q_norm_w,
              fox_k_norm_w, fox_o_norm_w, w_out, norm_ffn_w, ffn_w_up, ffn_conv_w, ffn_conv_b,
              ffn_w_down, norm_final_w):
    in_dtype = x.dtype
    for l in range(DEPTH):
        h = rmsnorm(x, norm_mix_w[l])
        p = h @ w_in[l].astype(jnp.float32)
        y_rwkv = rwkv7_group(p[..., :RWKV_COLS], rwkv_mu[l], rwkv_w0[l], rwkv_w2[l], rwkv_a0[l],
                             rwkv_a2[l], rwkv_g2[l], rwkv_k_k[l], rwkv_k_a[l], rwkv_r_k[l],
                             rwkv_lnx_w[l], rwkv_lnx_b[l])
        y_fox = fox_group(p[..., RWKV_COLS:], fox_f_bias[l], fox_q_norm_w[l], fox_k_norm_w[l],
                          fox_o_norm_w[l])
        x = x + jnp.concatenate([y_rwkv, y_fox], axis=-1) @ w_out[l].astype(jnp.float32)
        h = rmsnorm(x, norm_ffn_w[l])
        x = x + conv_glu_ffn(h, ffn_w_up[l].astype(jnp.float32), ffn_conv_w[l], ffn_conv_b[l],
                             ffn_w_down[l].astype(jnp.float32))
    return rmsnorm(x, norm_final_w).astype(in_dtype)
```

```python
import functools

import jax
import jax.numpy as jnp
from jax import lax
from jax.experimental import pallas as pl
from jax.experimental.pallas import tpu as pltpu

F32 = jnp.float32
BF16 = jnp.bfloat16

D_MODEL = 1024
HEAD_DIM = 64
N_HEADS = 8
MIX_HALF = N_HEADS * HEAD_DIM
DECAY_LORA = 64
AAA_LORA = 64
GATE_LORA = 128
RWKV_COLS = 3 * MIX_HALF + DECAY_LORA + AAA_LORA + GATE_LORA
D_FF = 2816
NORM_EPS = 1e-6
LNX_EPS = 64e-5

LANES = 128
PAIRS = MIX_HALF // LANES
CHUNK = 64
P_COLS = 4096
FL_BLOCK = RWKV_COLS // LANES
FOX_Q_COL = 2048
FF_TILE = 256
VMEM_LIMIT = 56 * 1024 * 1024


def _params(sem):
    return pltpu.CompilerParams(dimension_semantics=sem, vmem_limit_bytes=VMEM_LIMIT)


def _split3(x):
    hi = x.astype(BF16)
    r1 = x - hi.astype(F32)
    mid = r1.astype(BF16)
    lo = (r1 - mid.astype(F32)).astype(BF16)
    return hi, mid, lo


def _dot(a, b, dims=None):
    if dims is None:
        dims = (((a.ndim - 1,), (0,)), ((), ()))
    return lax.dot_general(a, b, dims, preferred_element_type=F32)


_NT = (((1,), (1,)), ((), ()))
_TN = (((0,), (0,)), ((), ()))


def _mm(a, b, dims=None, passes=1):
    ah = a.astype(BF16)
    bh = b.astype(BF16)
    out = _dot(ah, bh, dims)
    if passes == 1:
        return out
    al = (a - ah.astype(F32)).astype(BF16)
    bl = (b - bh.astype(F32)).astype(BF16)
    return out + _dot(ah, bl, dims) + _dot(al, bh, dims)


def _dot_exact_rhs(x, rhs_bf16):
    hi, mid, lo = _split3(x)
    return _dot(hi, rhs_bf16) + _dot(mid, rhs_bf16) + _dot(lo, rhs_bf16)


def _iota(shape, dim):
    return lax.broadcasted_iota(jnp.int32, shape, dim)


def _head_ones():
    r = _iota((LANES, LANES), 0) // HEAD_DIM
    c = _iota((LANES, LANES), 1) // HEAD_DIM
    return jnp.where(r == c, 1.0, 0.0).astype(BF16)


def _block_tri(n):
    r = _iota((n, n), 0)
    c = _iota((n, n), 1)
    return jnp.where((r // CHUNK == c // CHUNK) & (c <= r), 1.0, 0.0).astype(BF16)


def _sigmoid(z):
    return 1.0 / (1.0 + jnp.exp(-z))


def _softplus(z):
    return jnp.maximum(z, 0.0) + jnp.log1p(jnp.exp(-jnp.abs(z)))


def _inproj_kernel(x_ref, nw_ref, w_ref, o_ref, h_ref):
    @pl.when(pl.program_id(1) == 0)
    def _():
        x = x_ref[...]
        ms = jnp.mean(x * x, axis=-1, keepdims=True)
        h_ref[...] = (x * lax.rsqrt(ms + NORM_EPS) * nw_ref[...]).astype(BF16)

    o_ref[...] = _dot(h_ref[...], w_ref[...])


def _inproj(x2d, norm_w, w_packed, tm, tn):
    m = x2d.shape[0]
    return pl.pallas_call(
        _inproj_kernel,
        grid=(m // tm, P_COLS // tn),
        in_specs=[
            pl.BlockSpec((tm, D_MODEL), lambda i, j: (i, 0)),
            pl.BlockSpec((1, D_MODEL), lambda i, j: (0, 0)),
            pl.BlockSpec((D_MODEL, tn), lambda i, j: (0, j)),
        ],
        out_specs=pl.BlockSpec((tm, tn), lambda i, j: (i, j)),
        out_shape=jax.ShapeDtypeStruct((m, P_COLS), F32),
        scratch_shapes=[pltpu.VMEM((tm, D_MODEL), BF16)],
        compiler_params=_params(("arbitrary", "arbitrary")),
        name="inproj",
    )(x2d, norm_w, w_packed)


def _rwkv_kernel(r_ref, k_ref, v_ref, wa_ref, gl_ref,
                 mur_ref, muk_ref, muv_ref, muwa_ref, mug_ref,
                 w0_ref, a0_ref, kk_ref, ka_ref, rk_ref, lnw_ref, lnb_ref,
                 w2_ref, a2_ref, g2_ref,
                 y_ref, prev_ref, s_ref, *, tt):
    tile = pl.program_id(2)

    @pl.when(tile == 0)
    def _():
        prev_ref[...] = jnp.zeros_like(prev_ref)
        s_ref[...] = jnp.zeros_like(s_ref)

    row = _iota((tt, LANES), 0)
    lane = _iota((1, LANES), 1)
    first_row = row == 0

    def shifted(idx, x_ref, mu_ref):
        x = x_ref[...]
        prev = jnp.where(first_row, prev_ref[idx, 7:8, :], pltpu.roll(x, 1, axis=0))
        prev_ref[idx] = x[tt - 8:, :]
        return x + (prev - x) * mu_ref[...]

    r = shifted(0, r_ref, mur_ref)
    k = shifted(1, k_ref, muk_ref)
    v = shifted(2, v_ref, muv_ref)
    wa = shifted(3, wa_ref, muwa_ref)
    gl = shifted(4, gl_ref, mug_ref)

    ones_h = _head_ones()

    w_lora = _dot(jnp.tanh(wa).astype(BF16), w2_ref[...])
    a_lora = _dot(wa.astype(BF16), a2_ref[...])
    w = -_softplus(-(w0_ref[...] + w_lora)) - 0.5
    logd = -jnp.exp(w)
    a_sig = _sigmoid(a0_ref[...] + a_lora)
    g = _dot(_sigmoid(gl).astype(BF16), g2_ref[...])

    kk = k * kk_ref[...]
    kk = kk * lax.rsqrt(jnp.maximum(_dot_exact_rhs(kk * kk, ones_h), 1e-24))
    k2 = k * (1.0 + (a_sig - 1.0) * ka_ref[...])
    bonus = _dot_exact_rhs(r * k2 * rk_ref[...], ones_h) * v

    cs = _dot_exact_rhs_left(_block_tri(tt), logd)
    w_incl = jnp.exp(cs)
    a_t = (-kk) * jnp.exp(cs - logd)
    w_inv = jnp.exp(-cs)
    b_t = kk * a_sig * w_inv
    k_t = k2 * w_inv
    r_t = r * w_incl

    m0 = lane < HEAD_DIM
    ri = _iota((LANES, LANES), 0)
    ci = _iota((LANES, LANES), 1)
    same = (ri // CHUNK) == (ci // CHUNK)
    strict = same & (ci < ri)
    incl = same & (ci <= ri)
    eye = jnp.where(ri == ci, 1.0, 0.0).astype(F32)

    def stack(x):
        return jnp.concatenate([jnp.where(m0, x, 0.0), jnp.where(m0, 0.0, x)], axis=0)

    s = s_ref[...]
    ys = []
    for c in range(tt // CHUNK):
        sl = slice(c * CHUNK, (c + 1) * CHUNK)
        a_st, r_st, b_st, k_st, v_st = (stack(x[sl]) for x in (a_t, r_t, b_t, k_t, v))
        bk = jnp.concatenate([b_st, k_st], axis=0)
        sc = _mm(jnp.concatenate([a_st, r_st], axis=0), bk, _NT)
        l_ab = jnp.where(strict, sc[:LANES, :LANES], 0.0)
        l_ak = jnp.where(strict, sc[:LANES, LANES:], 0.0)
        m_rb = jnp.where(incl, sc[LANES:, :LANES], 0.0)
        m_rk = jnp.where(incl, sc[LANES:, LANES:], 0.0)

        t_inv = eye + l_ab
        lp = l_ab
        for _ in range(CHUNK.bit_length() - 2):
            lp = _mm(lp, lp, passes=3)
            t_inv = t_inv + _mm(t_inv, lp, passes=3)

        au = _mm(t_inv, jnp.concatenate([a_st, _mm(l_ak, v_st)], axis=1), passes=3)
        a2_st = au[:, :LANES]
        u0_st = au[:, LANES:]
        r2_st = r_st + _mm(m_rb, a2_st)
        y0_st = _mm(m_rb, u0_st) + _mm(m_rk, v_st)
        wc = w_incl[(c + 1) * CHUNK - 1:(c + 1) * CHUNK, :]
        g_mat = (eye + _mm(a2_st, b_st, _TN)) * wc
        h_mat = _mm(jnp.concatenate([u0_st, v_st], axis=0), bk, _TN) * wc

        y_st = _mm(r2_st, s, _NT, passes=3) + y0_st
        s = _mm(s, g_mat, passes=3) + h_mat
        ys.append(y_st[:CHUNK] + y_st[CHUNK:])
    s_ref[...] = s
    y = jnp.concatenate(ys, axis=0)

    mean = _dot_exact_rhs(y, ones_h) * (1.0 / HEAD_DIM)
    yc = y - mean
    var = _dot_exact_rhs(yc * yc, ones_h) * (1.0 / HEAD_DIM)
    yn = yc * lax.rsqrt(var + LNX_EPS) * lnw_ref[...] + lnb_ref[...]
    y_ref[...] = (yn + bonus) * g


def _dot_exact_rhs_left(lhs_bf16, x):
    hi, mid, lo = _split3(x)
    return _dot(lhs_bf16, hi) + _dot(lhs_bf16, mid) + _dot(lhs_bf16, lo)


def _rwkv(p3, mu, pair_params, w2p, a2p, g2, tt):
    b, t, _ = p3.shape

    def tok(col):
        return pl.BlockSpec((None, tt, LANES), lambda bi, hp, ti, col=col: (bi, ti, col(hp)))

    def mu_spec(col):
        return pl.BlockSpec((None, 1, LANES), lambda bi, hp, ti, col=col: (col(hp), 0, 0))

    pair_spec = pl.BlockSpec((None, 1, LANES), lambda bi, hp, ti: (hp, 0, 0))
    cols = [lambda hp: hp, lambda hp: PAIRS + hp, lambda hp: 2 * PAIRS + hp,
            lambda hp: 3 * PAIRS, lambda hp: 3 * PAIRS + 1]
    lora_spec = pl.BlockSpec((LANES, LANES), lambda bi, hp, ti: (0, hp))
    return pl.pallas_call(
        functools.partial(_rwkv_kernel, tt=tt),
        grid=(b, PAIRS, t // tt),
        in_specs=[tok(c) for c in cols] + [mu_spec(c) for c in cols]
        + [pair_spec] * len(pair_params) + [lora_spec] * 3,
        out_specs=pl.BlockSpec((None, tt, LANES), lambda bi, hp, ti: (bi, ti, hp)),
        out_shape=jax.ShapeDtypeStruct((b, t, MIX_HALF), F32),
        scratch_shapes=[pltpu.VMEM((5, 8, LANES), F32), pltpu.VMEM((LANES, LANES), F32)],
        compiler_params=_params(("arbitrary", "arbitrary", "arbitrary")),
        name="rwkv7",
    )(*([p3] * 5), *([mu] * 5), *pair_params, w2p, a2p, g2)


def _fox_prep_kernel(q_ref, k_ref, v_ref, fl_ref, fb_ref, qw_ref, kw_ref,
                     qa_ref, ka_ref, va_ref, carry_ref, *, tt):
    @pl.when(pl.program_id(1) == 0)
    def _():
        carry_ref[...] = jnp.zeros_like(carry_ref)

    lane = _iota((1, LANES), 1)
    ones_h = _head_ones()
    r_t = _iota((tt, tt), 0)
    c_t = _iota((tt, tt), 1)
    tri = jnp.where(c_t <= r_t, 1.0, 0.0).astype(BF16)

    logf = -_softplus(-(fl_ref[...] + fb_ref[...]))
    c = _dot_exact_rhs_left(tri, logf) + carry_ref[0:1, :]
    carry_ref[...] = jnp.broadcast_to(c[tt - 1:tt, :], carry_ref.shape)
    c_parts = _split3(c)

    sel_r = _iota((LANES, LANES), 0)
    for pb in range(PAIRS):
        sl = slice(pb * LANES, (pb + 1) * LANES)
        q = q_ref[:, sl]
        k = k_ref[:, sl]
        v = v_ref[:, sl]
        qn = q * lax.rsqrt(_dot_exact_rhs(q * q, ones_h) * (1.0 / HEAD_DIM) + NORM_EPS) * qw_ref[...]
        kn = k * lax.rsqrt(_dot_exact_rhs(k * k, ones_h) * (1.0 / HEAD_DIM) + NORM_EPS) * kw_ref[...]
        qn = qn * (HEAD_DIM ** -0.5)
        for half in range(2):
            h = 2 * pb + half
            own = (lane // HEAD_DIM) == half
            a0 = HEAD_DIM * (1 - half)
            sel = jnp.where(sel_r == h, 1.0, 0.0).astype(BF16)
            cb = [_dot(part, sel) for part in c_parts]
            qa = jnp.where(own, qn, 0.0)
            ka = jnp.where(own, kn, 0.0)
            for i in range(3):
                qa = jnp.where(lane == a0 + i, cb[i], qa)
                ka = jnp.where(lane == a0 + 3 + i, -cb[i], ka)
            qa = jnp.where((lane >= a0 + 3) & (lane < a0 + 6), 1.0, qa)
            ka = jnp.where((lane >= a0) & (lane < a0 + 3), 1.0, ka)
            va = jnp.where(own, v, jnp.where(lane == a0, 1.0, 0.0))
            qa_ref[h] = qa.astype(BF16)
            ka_ref[h] = ka.astype(BF16)
            va_ref[h] = va.astype(BF16)


def _fox_prep(p3, f_bias, qw, kw, tt):
    b, t, _ = p3.shape
    wide = MIX_HALF // LANES * LANES

    def tok(cb):
        return pl.BlockSpec((None, tt, wide), lambda bi, ti, cb=cb: (bi, ti, cb))

    vec = pl.BlockSpec((1, LANES), lambda bi, ti: (0, 0))
    out_spec = pl.BlockSpec((None, N_HEADS, tt, LANES), lambda bi, ti: (bi, 0, ti, 0))
    out_shape = jax.ShapeDtypeStruct((b, N_HEADS, t, LANES), BF16)
    qcb = FOX_Q_COL // MIX_HALF
    return pl.pallas_call(
        functools.partial(_fox_prep_kernel, tt=tt),
        grid=(b, t // tt),
        in_specs=[tok(qcb), tok(qcb + 1), tok(qcb + 2),
                  pl.BlockSpec((None, tt, LANES), lambda bi, ti: (bi, ti, FL_BLOCK)),
                  vec, vec, vec],
        out_specs=[out_spec] * 3,
        out_shape=[out_shape] * 3,
        scratch_shapes=[pltpu.VMEM((8, LANES), F32)],
        compiler_params=_params(("arbitrary", "arbitrary")),
        name="fox_prep",
    )(p3, p3, p3, p3, f_bias, qw, kw)


def _fox_attn_kernel(qa_ref, ka_ref, va_ref, og_ref, ow_ref, y_ref, *, tq):
    i = pl.program_id(2)
    lane = _iota((1, LANES), 1)
    r_q = _iota((tq, tq), 0)
    c_k = _iota((tq, tq), 1)
    causal = c_k <= r_q
    halves = []
    for half in range(2):
        q = qa_ref[half]

        def step(j, carry, masked, half=half, q=q):
            m, acc = carry
            start = pl.multiple_of(j * tq, tq)
            k = ka_ref[half, pl.ds(start, tq), :]
            v = va_ref[half, pl.ds(start, tq), :]
            s = _dot(q, k, _NT)
            if masked:
                s = jnp.where(causal, s, -jnp.inf)
            m_new = jnp.maximum(m, jnp.max(s, axis=1, keepdims=True))
            p = jnp.exp(s - m_new)
            acc = jnp.exp(m - m_new) * acc + _dot(p.astype(BF16), v)
            return m_new, acc

        carry = (jnp.full((tq, 1), -jnp.inf, F32), jnp.zeros((tq, LANES), F32))
        carry = lax.fori_loop(0, i, functools.partial(step, masked=False), carry)
        _, acc = step(i, carry, True)
        denom = jnp.sum(jnp.where(lane == HEAD_DIM * (1 - half), acc, 0.0), axis=1, keepdims=True)
        halves.append(acc / denom)
    o = jnp.where(lane < HEAD_DIM, halves[0], halves[1])
    ms = _dot_exact_rhs(o * o, _head_ones()) * (1.0 / HEAD_DIM)
    y_ref[...] = o * lax.rsqrt(ms + NORM_EPS) * ow_ref[...] * _sigmoid(og_ref[...])


def _fox_attn(qa, ka, va, p3, ow, tq):
    b, _, t, _ = qa.shape
    og_cb = (FOX_Q_COL + 3 * MIX_HALF) // LANES
    kv_spec = pl.BlockSpec((None, 2, t, LANES), lambda bi, hp, qi: (bi, hp, 0, 0))
    return pl.pallas_call(
        functools.partial(_fox_attn_kernel, tq=tq),
        grid=(b, PAIRS, t // tq),
        in_specs=[pl.BlockSpec((None, 2, tq, LANES), lambda bi, hp, qi: (bi, hp, qi, 0)),
                  kv_spec, kv_spec,
                  pl.BlockSpec((None, tq, LANES), lambda bi, hp, qi: (bi, qi, og_cb + hp)),
                  pl.BlockSpec((1, LANES), lambda bi, hp, qi: (0, 0))],
        out_specs=pl.BlockSpec((None, tq, LANES), lambda bi, hp, qi: (bi, qi, hp)),
        out_shape=jax.ShapeDtypeStruct((b, t, MIX_HALF), F32),
        compiler_params=_params(("arbitrary", "arbitrary", "arbitrary")),
        name="fox_attn",
    )(qa, ka, va, p3, ow)


def _ffn_kernel(x_ref, yr_ref, yf_ref, wo_ref, nf_ref, wug_ref, wuv_ref, cwg_ref, cwv_ref,
                cbg_ref, cbv_ref, wd_ref, nfin_ref, o_ref,
                x2_ref, h2_ref, acc_ref, ug_ref, uv_ref, carry_ref, *, tm):
    t = pl.program_id(1)
    j = pl.program_id(2)

    @pl.when(j == 0)
    def _():
        x2 = (x_ref[...] + _dot(yr_ref[...].astype(BF16), wo_ref[:MIX_HALF, :])
              + _dot(yf_ref[...].astype(BF16), wo_ref[MIX_HALF:, :]))
        x2_ref[...] = x2
        ms = jnp.mean(x2 * x2, axis=-1, keepdims=True)
        h2_ref[...] = (x2 * lax.rsqrt(ms + NORM_EPS) * nf_ref[...]).astype(BF16)
        acc_ref[...] = jnp.zeros_like(acc_ref)

    @pl.when(t == 0)
    def _():
        carry_ref[j] = jnp.zeros(carry_ref.shape[1:], F32)

    def conv(which, u_ref, w_ref, cw_ref, cb_ref):
        u = _dot(h2_ref[...], w_ref[...])
        u_ref[0:8, :] = carry_ref[j, which]
        u_ref[8:8 + tm, :] = u
        carry_ref[j, which] = u[tm - 8:, :]
        return (cw_ref[0:1, :] * u_ref[6:6 + tm, :] + cw_ref[1:2, :] * u_ref[7:7 + tm, :]
                + cw_ref[2:3, :] * u + cb_ref[...])

    gate = conv(0, ug_ref, wug_ref, cwg_ref, cbg_ref)
    val = conv(1, uv_ref, wuv_ref, cwv_ref, cbv_ref)
    hidden = gate * _sigmoid(gate) * val
    acc_ref[...] += _dot(hidden.astype(BF16), wd_ref[...])

    @pl.when(j == pl.num_programs(2) - 1)
    def _():
        xo = x2_ref[...] + acc_ref[...]
        ms = jnp.mean(xo * xo, axis=-1, keepdims=True)
        o_ref[...] = xo * lax.rsqrt(ms + NORM_EPS) * nfin_ref[...]


def _ffn(x3, yr, yf, wo, nf, wu, cw, cb, wd, nfin, tm):
    b, t, _ = x3.shape
    nj = D_FF // FF_TILE
    row = lambda w: pl.BlockSpec((None, tm, w), lambda bi, ti, j: (bi, ti, 0))
    full = lambda shape: pl.BlockSpec(shape, lambda bi, ti, j: (0, 0))
    gate_col = lambda rows: pl.BlockSpec((rows, FF_TILE), lambda bi, ti, j: (0, j))
    val_col = lambda rows: pl.BlockSpec((rows, FF_TILE), lambda bi, ti, j: (0, nj + j))
    return pl.pallas_call(
        functools.partial(_ffn_kernel, tm=tm),
        grid=(b, t // tm, nj),
        in_specs=[row(D_MODEL), row(MIX_HALF), row(MIX_HALF),
                  full((D_MODEL, D_MODEL)), full((1, D_MODEL)),
                  gate_col(D_MODEL), val_col(D_MODEL), gate_col(3), val_col(3), gate_col(1), val_col(1),
                  pl.BlockSpec((FF_TILE, D_MODEL), lambda bi, ti, j: (j, 0)),
                  full((1, D_MODEL))],
        out_specs=row(D_MODEL),
        out_shape=jax.ShapeDtypeStruct(x3.shape, F32),
        scratch_shapes=[pltpu.VMEM((tm, D_MODEL), F32), pltpu.VMEM((tm, D_MODEL), BF16),
                        pltpu.VMEM((tm, D_MODEL), F32),
                        pltpu.VMEM((tm + 8, FF_TILE), F32), pltpu.VMEM((tm + 8, FF_TILE), F32),
                        pltpu.VMEM((nj, 2, 8, FF_TILE), F32)],
        compiler_params=_params(("arbitrary", "arbitrary", "arbitrary")),
        name="outproj_convffn",
    )(x3, yr, yf, wo, nf, wu, wu, cw, cw, cb, cb, wd, nfin)


def _pack_w_in(w_in):
    fox = w_in[:, RWKV_COLS:]
    fl = fox[:, 4 * MIX_HALF:]
    pad = jnp.zeros((D_MODEL, FOX_Q_COL - RWKV_COLS - fl.shape[1]), w_in.dtype)
    return jnp.concatenate([w_in[:, :RWKV_COLS], fl, pad, fox[:, :4 * MIX_HALF]], axis=1).astype(BF16)


def _tile(n, pref):
    t = min(n, pref)
    assert n % t == 0, (n, t)
    return t


def kernel(x, norm_mix_w, w_in, rwkv_mu, rwkv_w0, rwkv_w2, rwkv_a0, rwkv_a2, rwkv_g2, rwkv_k_k, rwkv_k_a,
           rwkv_r_k, rwkv_lnx_w, rwkv_lnx_b, fox_f_bias, fox_q_norm_w, fox_k_norm_w, fox_o_norm_w, w_out,
           norm_ffn_w, ffn_w_up, ffn_conv_w, ffn_conv_b, ffn_w_down, norm_final_w):
    b, t, d = x.shape
    assert d == D_MODEL and norm_mix_w.shape[0] == 1 and t % CHUNK == 0
    m = b * t
    row = lambda a: a.reshape(1, -1).astype(F32)
    pair = lambda a: a.reshape(PAIRS, 1, LANES).astype(F32)
    twice = lambda a: jnp.tile(a.reshape(1, HEAD_DIM), (1, 2)).astype(F32)

    p = _inproj(x.reshape(m, d), row(norm_mix_w[0]), _pack_w_in(w_in[0]), _tile(m, 1024), 1024)
    p3 = p.reshape(b, t, P_COLS)

    zeros = jnp.zeros((DECAY_LORA, MIX_HALF), F32)
    w2p = jnp.concatenate([rwkv_w2[0], zeros], axis=0).astype(BF16)
    a2p = jnp.concatenate([zeros, rwkv_a2[0]], axis=0).astype(BF16)
    pair_params = [pair(a[0]) for a in (rwkv_w0, rwkv_a0, rwkv_k_k, rwkv_k_a, rwkv_r_k, rwkv_lnx_w, rwkv_lnx_b)]
    mu = rwkv_mu[0].reshape(RWKV_COLS // LANES, 1, LANES).astype(F32)
    y_rwkv = _rwkv(p3, mu, pair_params, w2p, a2p, rwkv_g2[0].astype(BF16), _tile(t, 256))

    fb = jnp.zeros((1, LANES), F32).at[0, :N_HEADS].set(fox_f_bias[0])
    qa, ka, va = _fox_prep(p3, fb, twice(fox_q_norm_w[0]), twice(fox_k_norm_w[0]), _tile(t, 256))
    y_fox = _fox_attn(qa, ka, va, p3, twice(fox_o_norm_w[0]), _tile(t, 256))

    out = _ffn(x, y_rwkv, y_fox, w_out[0].astype(BF16), row(norm_ffn_w[0]), ffn_w_up[0].astype(BF16),
               ffn_conv_w[0].astype(F32), row(ffn_conv_b[0]), ffn_w_down[0].astype(BF16), row(norm_final_w),
               _tile(t, 512))
    return out.astype(x.dtype)
```

```python
import functools

import jax
import jax.numpy as jnp
from jax import lax
from jax.experimental import pallas as pl
from jax.experimental.pallas import tpu as pltpu

F32 = jnp.float32
BF16 = jnp.bfloat16

D_MODEL = 1024
HEAD_DIM = 64
N_HEADS = 8
MIX_HALF = N_HEADS * HEAD_DIM
DECAY_LORA = 64
AAA_LORA = 64
GATE_LORA = 128
RWKV_COLS = 3 * MIX_HALF + DECAY_LORA + AAA_LORA + GATE_LORA
D_FF = 2816
NORM_EPS = 1e-6
LNX_EPS = 64e-5

LANES = 128
PAIRS = MIX_HALF // LANES
CHUNK = 64
P_COLS = 4096
FL_BLOCK = RWKV_COLS // LANES
FOX_Q_COL = 2048
FF_TILE = 256
VMEM_LIMIT = 56 * 1024 * 1024


def _params(sem):
    return pltpu.CompilerParams(dimension_semantics=sem, vmem_limit_bytes=VMEM_LIMIT)


def _split3(x):
    hi = x.astype(BF16)
    r1 = x - hi.astype(F32)
    mid = r1.astype(BF16)
    lo = (r1 - mid.astype(F32)).astype(BF16)
    return hi, mid, lo


def _dot(a, b, dims=None):
    if dims is None:
        dims = (((a.ndim - 1,), (0,)), ((), ()))
    return lax.dot_general(a, b, dims, preferred_element_type=F32)


_NT = (((1,), (1,)), ((), ()))

_BNN = (((2,), (1,)), ((0,), (0,)))
_BNT = (((2,), (2,)), ((0,), (0,)))
_BTN = (((1,), (1,)), ((0,), (0,)))


def _bmm(a, b, dims=_BNN):
    return lax.dot_general(a.astype(BF16), b.astype(BF16), dims, preferred_element_type=F32)


def _dot_exact_rhs(x, rhs_bf16):
    hi, mid, lo = _split3(x)
    return _dot(hi, rhs_bf16) + _dot(mid, rhs_bf16) + _dot(lo, rhs_bf16)


def _dot_exact_lhs(lhs_bf16, x):
    hi, mid, lo = _split3(x)
    return _dot(lhs_bf16, hi) + _dot(lhs_bf16, mid) + _dot(lhs_bf16, lo)


def _iota(shape, dim):
    return lax.broadcasted_iota(jnp.int32, shape, dim)


def _head_ones():
    r = _iota((LANES, LANES), 0) // HEAD_DIM
    c = _iota((LANES, LANES), 1) // HEAD_DIM
    return jnp.where(r == c, 1.0, 0.0).astype(BF16)


def _head_sum(x, ones_h):
    xb = x.astype(BF16)
    return jnp.concatenate([_dot(xb[:, i:i + LANES], ones_h) for i in range(0, x.shape[1], LANES)], axis=1)


def _block_tri(n):
    r = _iota((n, n), 0)
    c = _iota((n, n), 1)
    return jnp.where((r // CHUNK == c // CHUNK) & (c <= r), 1.0, 0.0).astype(BF16)


def _sigmoid(z):
    return 1.0 / (1.0 + jnp.exp(-z))


def _softplus(z):
    return jnp.maximum(z, 0.0) + jnp.log1p(jnp.exp(-jnp.abs(z)))


def _inproj_kernel(x_ref, nw_ref, w_ref, o_ref, h_ref):
    @pl.when(pl.program_id(1) == 0)
    def _():
        x = x_ref[...]
        ms = jnp.mean(x * x, axis=-1, keepdims=True)
        h_ref[...] = (x * lax.rsqrt(ms + NORM_EPS) * nw_ref[...]).astype(BF16)

    o_ref[...] = _dot(h_ref[...], w_ref[...])


def _inproj(x2d, norm_w, w_packed, tm, tn):
    m = x2d.shape[0]
    return pl.pallas_call(
        _inproj_kernel,
        grid=(m // tm, P_COLS // tn),
        in_specs=[
            pl.BlockSpec((tm, D_MODEL), lambda i, j: (i, 0)),
            pl.BlockSpec((1, D_MODEL), lambda i, j: (0, 0)),
            pl.BlockSpec((D_MODEL, tn), lambda i, j: (0, j)),
        ],
        out_specs=pl.BlockSpec((tm, tn), lambda i, j: (i, j)),
        out_shape=jax.ShapeDtypeStruct((m, P_COLS), F32),
        scratch_shapes=[pltpu.VMEM((tm, D_MODEL), BF16)],
        compiler_params=_params(("arbitrary", "arbitrary")),
        name="inproj",
    )(x2d, norm_w, w_packed)


def _rwkv_kernel(r_ref, k_ref, v_ref, wa_ref, gl_ref,
                 mur_ref, muk_ref, muv_ref, muwa_ref, mug_ref,
                 w0_ref, a0_ref, kk_ref, ka_ref, rk_ref, lnw_ref, lnb_ref,
                 w2_ref, a2_ref, g2_ref,
                 y_ref, pr_ref, pk_ref, pv_ref, pwa_ref, pg_ref, s_ref, *, tt):
    @pl.when(pl.program_id(1) == 0)
    def _():
        for ref in (pr_ref, pk_ref, pv_ref, pwa_ref, pg_ref, s_ref):
            ref[...] = jnp.zeros_like(ref)

    nc = tt // CHUNK

    def shifted(x_ref, prev_ref, mu_ref):
        x = x_ref[...]
        first_row = _iota(x.shape, 0) == 0
        prev = jnp.where(first_row, prev_ref[7:8, :], pltpu.roll(x, 1, axis=0))
        prev_ref[...] = x[tt - 8:, :]
        return x + (prev - x) * mu_ref[...]

    r = shifted(r_ref, pr_ref, mur_ref)
    k = shifted(k_ref, pk_ref, muk_ref)
    v = shifted(v_ref, pv_ref, muv_ref)
    wa = shifted(wa_ref, pwa_ref, muwa_ref)
    gl = shifted(gl_ref, pg_ref, mug_ref)

    ones_h = _head_ones()

    w_lora = _dot(jnp.tanh(wa).astype(BF16), w2_ref[...])
    a_lora = _dot(wa.astype(BF16), a2_ref[...])
    w = -_softplus(-(w0_ref[...] + w_lora)) - 0.5
    logd = -jnp.exp(w)
    a_sig = _sigmoid(a0_ref[...] + a_lora)
    g = _dot(_sigmoid(gl).astype(BF16), g2_ref[...])

    kk = k * kk_ref[...]
    kk = kk * lax.rsqrt(jnp.maximum(_head_sum(kk * kk, ones_h), 1e-24))
    k2 = k * (1.0 + (a_sig - 1.0) * ka_ref[...])
    bonus = _head_sum(r * k2 * rk_ref[...], ones_h) * v

    cs = _dot_exact_lhs(_block_tri(tt), logd)
    w_incl = jnp.exp(cs)
    w_inv = jnp.exp(-cs)
    a_t = (-kk) * jnp.exp(cs - logd)
    b_t = kk * a_sig * w_inv
    k_t = k2 * w_inv
    r_t = r * w_incl

    m0 = _iota((1, 1, LANES), 2) < HEAD_DIM

    def units(x):
        out = []
        for p in range(PAIRS):
            xp = x[:, p * LANES:(p + 1) * LANES].reshape(nc, CHUNK, LANES)
            out.append(jnp.concatenate([jnp.where(m0, xp, 0.0), jnp.where(m0, 0.0, xp)], axis=1))
        return jnp.concatenate(out, axis=0).astype(BF16)

    a_st, r_st, b_st, k_st, v_st = (units(x) for x in (a_t, r_t, b_t, k_t, v))
    wc = jnp.concatenate(
        [w_incl[:, p * LANES:(p + 1) * LANES].reshape(nc, CHUNK, LANES)[:, CHUNK - 1:, :] for p in range(PAIRS)],
        axis=0)

    ri = _iota((1, LANES, LANES), 1)
    ci = _iota((1, LANES, LANES), 2)
    same = (ri // CHUNK) == (ci // CHUNK)
    strict = same & (ci < ri)
    incl = same & (ci <= ri)
    eye = jnp.where(ri == ci, 1.0, 0.0).astype(F32)

    bk = jnp.concatenate([b_st, k_st], axis=1)
    sc = _bmm(jnp.concatenate([a_st, r_st], axis=1), bk, _BNT)
    l_ab = jnp.where(strict, sc[:, :LANES, :LANES], 0.0)
    l_ak = jnp.where(strict, sc[:, :LANES, LANES:], 0.0)
    m_rbk = jnp.concatenate([jnp.where(incl, sc[:, LANES:, :LANES], 0.0),
                             jnp.where(incl, sc[:, LANES:, LANES:], 0.0)], axis=2).astype(BF16)

    t_inv = eye + l_ab
    lp = l_ab.astype(BF16)
    lp = _bmm(lp, lp).astype(BF16)
    for _ in range(CHUNK.bit_length() - 3):
        both = _bmm(jnp.concatenate([lp, t_inv.astype(BF16)], axis=1), lp)
        t_inv = t_inv + both[:, LANES:]
        lp = both[:, :LANES].astype(BF16)
    t_inv = t_inv + _bmm(t_inv, lp)

    lakv = _bmm(l_ak, v_st)
    au = _bmm(t_inv, jnp.concatenate([a_st, lakv.astype(BF16)], axis=2))
    au_b = au.astype(BF16)
    zv = jnp.concatenate([jnp.zeros_like(v_st), v_st], axis=2)
    ry = _bmm(m_rbk, jnp.concatenate([au_b, zv], axis=1))
    r2_st = r_st.astype(F32) + ry[:, :, :LANES]
    y0_st = ry[:, :, LANES:]
    g_mat = (eye + _bmm(au_b[:, :, :LANES], b_st, _BTN)) * wc
    h_mat = _bmm(jnp.concatenate([au_b[:, :, LANES:], v_st], axis=1), bk, _BTN) * wc

    def chunk(x, c):
        return x.reshape((PAIRS, nc) + x.shape[1:])[:, c]

    s = s_ref[...]
    ys = []
    for c in range(nc):
        y_st = _bmm(chunk(r2_st, c), s, _BNT) + chunk(y0_st, c)
        s = _bmm(s, chunk(g_mat, c)) + chunk(h_mat, c)
        ys.append(y_st[:, :CHUNK] + y_st[:, CHUNK:])
    s_ref[...] = s
    y = jnp.concatenate([jnp.concatenate([yc[p] for yc in ys], axis=0) for p in range(PAIRS)], axis=1)

    mean = _head_sum(y, ones_h) * (1.0 / HEAD_DIM)
    yc = y - mean
    var = _head_sum(yc * yc, ones_h) * (1.0 / HEAD_DIM)
    yn = yc * lax.rsqrt(var + LNX_EPS) * lnw_ref[...] + lnb_ref[...]
    y_ref[...] = (yn + bonus) * g


def _rwkv(p3, mu, vec_params, w2p, a2p, g2, tt):
    b, t, _ = p3.shape
    nblk = MIX_HALF // LANES
    wide = lambda cb: pl.BlockSpec((None, tt, MIX_HALF), lambda bi, ti, cb=cb: (bi, ti, cb))
    narrow = lambda cb: pl.BlockSpec((None, tt, LANES), lambda bi, ti, cb=cb: (bi, ti, cb))
    mu_wide = lambda cb: pl.BlockSpec((1, MIX_HALF), lambda bi, ti, cb=cb: (0, cb))
    mu_narrow = lambda cb: pl.BlockSpec((1, LANES), lambda bi, ti, cb=cb: (0, cb))
    vec = pl.BlockSpec((1, MIX_HALF), lambda bi, ti: (0, 0))
    lora = pl.BlockSpec((LANES, MIX_HALF), lambda bi, ti: (0, 0))
    return pl.pallas_call(
        functools.partial(_rwkv_kernel, tt=tt),
        grid=(b, t // tt),
        in_specs=[wide(0), wide(1), wide(2), narrow(3 * nblk), narrow(3 * nblk + 1),
                  mu_wide(0), mu_wide(1), mu_wide(2), mu_narrow(3 * nblk), mu_narrow(3 * nblk + 1)]
        + [vec] * len(vec_params) + [lora] * 3,
        out_specs=pl.BlockSpec((None, tt, MIX_HALF), lambda bi, ti: (bi, ti, 0)),
        out_shape=jax.ShapeDtypeStruct((b, t, MIX_HALF), F32),
        scratch_shapes=[pltpu.VMEM((8, MIX_HALF), F32)] * 3 + [pltpu.VMEM((8, LANES), F32)] * 2
        + [pltpu.VMEM((PAIRS, LANES, LANES), F32)],
        compiler_params=_params(("arbitrary", "arbitrary")),
        name="rwkv7",
    )(*([p3] * 5), *([mu] * 5), *vec_params, w2p, a2p, g2)


def _fox_prep_kernel(q_ref, k_ref, v_ref, fl_ref, fb_ref, qw_ref, kw_ref,
                     qa_ref, ka_ref, va_ref, carry_ref, *, tt):
    @pl.when(pl.program_id(1) == 0)
    def _():
        carry_ref[...] = jnp.zeros_like(carry_ref)

    lane = _iota((1, LANES), 1)
    ones_h = _head_ones()
    r_t = _iota((tt, tt), 0)
    c_t = _iota((tt, tt), 1)
    tri = jnp.where(c_t <= r_t, 1.0, 0.0).astype(BF16)

    logf = -_softplus(-(fl_ref[...] + fb_ref[...]))
    c = _dot_exact_lhs(tri, logf) + carry_ref[0:1, :]
    carry_ref[...] = jnp.broadcast_to(c[tt - 1:tt, :], carry_ref.shape)
    c_parts = _split3(c)

    sel_r = _iota((LANES, LANES), 0)
    for pb in range(PAIRS):
        sl = slice(pb * LANES, (pb + 1) * LANES)
        q = q_ref[:, sl]
        k = k_ref[:, sl]
        v = v_ref[:, sl]
        qn = q * lax.rsqrt(_dot_exact_rhs(q * q, ones_h) * (1.0 / HEAD_DIM) + NORM_EPS) * qw_ref[...]
        kn = k * lax.rsqrt(_dot_exact_rhs(k * k, ones_h) * (1.0 / HEAD_DIM) + NORM_EPS) * kw_ref[...]
        qn = qn * (HEAD_DIM ** -0.5)
        for half in range(2):
            h = 2 * pb + half
            own = (lane // HEAD_DIM) == half
            a0 = HEAD_DIM * (1 - half)
            sel = jnp.where(sel_r == h, 1.0, 0.0).astype(BF16)
            cb = [_dot(part, sel) for part in c_parts]
            qa = jnp.where(own, qn, 0.0)
            ka = jnp.where(own, kn, 0.0)
            for i in range(3):
                qa = jnp.where(lane == a0 + i, cb[i], qa)
                ka = jnp.where(lane == a0 + 3 + i, -cb[i], ka)
            qa = jnp.where((lane >= a0 + 3) & (lane < a0 + 6), 1.0, qa)
            ka = jnp.where((lane >= a0) & (lane < a0 + 3), 1.0, ka)
            va = jnp.where(own, v, jnp.where(lane == a0, 1.0, 0.0))
            qa_ref[h] = qa.astype(BF16)
            ka_ref[h] = ka.astype(BF16)
            va_ref[h] = va.astype(BF16)


def _fox_prep(p3, f_bias, qw, kw, tt):
    b, t, _ = p3.shape

    def tok(cb):
        return pl.BlockSpec((None, tt, MIX_HALF), lambda bi, ti, cb=cb: (bi, ti, cb))

    vec = pl.BlockSpec((1, LANES), lambda bi, ti: (0, 0))
    out_spec = pl.BlockSpec((None, N_HEADS, tt, LANES), lambda bi, ti: (bi, 0, ti, 0))
    out_shape = jax.ShapeDtypeStruct((b, N_HEADS, t, LANES), BF16)
    qcb = FOX_Q_COL // MIX_HALF
    return pl.pallas_call(
        functools.partial(_fox_prep_kernel, tt=tt),
        grid=(b, t // tt),
        in_specs=[tok(qcb), tok(qcb + 1), tok(qcb + 2),
                  pl.BlockSpec((None, tt, LANES), lambda bi, ti: (bi, ti, FL_BLOCK)),
                  vec, vec, vec],
        out_specs=[out_spec] * 3,
        out_shape=[out_shape] * 3,
        scratch_shapes=[pltpu.VMEM((8, LANES), F32)],
        compiler_params=_params(("arbitrary", "arbitrary")),
        name="fox_prep",
    )(p3, p3, p3, p3, f_bias, qw, kw)


def _fox_attn_kernel(qa_ref, ka_ref, va_ref, og_ref, ow_ref, y_ref, m_ref, acc_ref, *, tq):
    i = pl.program_id(2)
    lane = _iota((1, LANES), 1)
    causal = _iota((tq, tq), 1) <= _iota((tq, tq), 0)
    m_ref[...] = jnp.full(m_ref.shape, -jnp.inf, F32)
    acc_ref[...] = jnp.zeros_like(acc_ref)
    q = [qa_ref[0], qa_ref[1]]

    def step(j, masked):
        start = pl.multiple_of(j * tq, tq)
        s = [_dot(q[h], ka_ref[h, pl.ds(start, tq), :], _NT) for h in range(2)]
        if masked:
            s = [jnp.where(causal, x, -jnp.inf) for x in s]
        m_old = [m_ref[h] for h in range(2)]
        m_new = [jnp.maximum(m_old[h], jnp.max(s[h], axis=1, keepdims=True)) for h in range(2)]
        p = [jnp.concatenate([jnp.exp(s[h][:, c:c + LANES] - m_new[h]) for c in range(0, tq, LANES)],
                             axis=1).astype(BF16) for h in range(2)]
        pv = [_dot(p[h], va_ref[h, pl.ds(start, tq), :]) for h in range(2)]
        for h in range(2):
            acc_ref[h] = jnp.exp(m_old[h] - m_new[h]) * acc_ref[h] + pv[h]
            m_ref[h] = m_new[h]

    def body(j, carry):
        step(j, False)
        return carry

    lax.fori_loop(0, i, body, 0)
    step(i, True)

    halves = []
    for h in range(2):
        acc = acc_ref[h]
        denom = jnp.sum(jnp.where(lane == HEAD_DIM * (1 - h), acc, 0.0), axis=1, keepdims=True)
        halves.append(acc / denom)
    o = jnp.where(lane < HEAD_DIM, halves[0], halves[1])
    ms = _dot_exact_rhs(o * o, _head_ones()) * (1.0 / HEAD_DIM)
    y_ref[...] = o * lax.rsqrt(ms + NORM_EPS) * ow_ref[...] * _sigmoid(og_ref[...])


def _fox_attn(qa, ka, va, p3, ow, tq):
    b, _, t, _ = qa.shape
    og_cb = (FOX_Q_COL + 3 * MIX_HALF) // LANES
    kv_spec = pl.BlockSpec((None, 2, t, LANES), lambda bi, hp, qi: (bi, hp, 0, 0))
    return pl.pallas_call(
        functools.partial(_fox_attn_kernel, tq=tq),
        grid=(b, PAIRS, t // tq),
        in_specs=[pl.BlockSpec((None, 2, tq, LANES), lambda bi, hp, qi: (bi, hp, qi, 0)),
                  kv_spec, kv_spec,
                  pl.BlockSpec((None, tq, LANES), lambda bi, hp, qi: (bi, qi, og_cb + hp)),
                  pl.BlockSpec((1, LANES), lambda bi, hp, qi: (0, 0))],
        out_specs=pl.BlockSpec((None, tq, LANES), lambda bi, hp, qi: (bi, qi, hp)),
        out_shape=jax.ShapeDtypeStruct((b, t, MIX_HALF), F32),
        scratch_shapes=[pltpu.VMEM((2, tq, LANES), F32), pltpu.VMEM((2, tq, LANES), F32)],
        compiler_params=_params(("arbitrary", "arbitrary", "arbitrary")),
        name="fox_attn",
    )(qa, ka, va, p3, ow)


def _ffn_kernel(x_ref, yr_ref, yf_ref, wo_ref, nf_ref, wug_ref, wuv_ref, cwg_ref, cwv_ref,
                cbg_ref, cbv_ref, wd_ref, nfin_ref, o_ref,
                x2_ref, h2_ref, acc_ref, ug_ref, uv_ref, carry_ref, *, tm):
    t = pl.program_id(1)
    j = pl.program_id(2)

    @pl.when(j == 0)
    def _():
        x2 = (x_ref[...] + _dot(yr_ref[...].astype(BF16), wo_ref[:MIX_HALF, :])
              + _dot(yf_ref[...].astype(BF16), wo_ref[MIX_HALF:, :]))
        x2_ref[...] = x2
        ms = jnp.mean(x2 * x2, axis=-1, keepdims=True)
        h2_ref[...] = (x2 * lax.rsqrt(ms + NORM_EPS) * nf_ref[...]).astype(BF16)
        acc_ref[...] = jnp.zeros_like(acc_ref)

    @pl.when(t == 0)
    def _():
        carry_ref[j] = jnp.zeros(carry_ref.shape[1:], F32)

    def conv(which, u_ref, w_ref, cw_ref, cb_ref):
        u = _dot(h2_ref[...], w_ref[...])
        u_ref[0:8, :] = carry_ref[j, which]
        u_ref[8:8 + tm, :] = u
        carry_ref[j, which] = u[tm - 8:, :]
        return (cw_ref[0:1, :] * u_ref[6:6 + tm, :] + cw_ref[1:2, :] * u_ref[7:7 + tm, :]
                + cw_ref[2:3, :] * u + cb_ref[...])

    gate = conv(0, ug_ref, wug_ref, cwg_ref, cbg_ref)
    val = conv(1, uv_ref, wuv_ref, cwv_ref, cbv_ref)
    hidden = gate * _sigmoid(gate) * val
    acc_ref[...] += _dot(hidden.astype(BF16), wd_ref[...])

    @pl.when(j == pl.num_programs(2) - 1)
    def _():
        xo = x2_ref[...] + acc_ref[...]
        ms = jnp.mean(xo * xo, axis=-1, keepdims=True)
        o_ref[...] = xo * lax.rsqrt(ms + NORM_EPS) * nfin_ref[...]


def _ffn(x3, yr, yf, wo, nf, wu, cw, cb, wd, nfin, tm):
    b, t, _ = x3.shape
    nj = D_FF // FF_TILE
    row = lambda w: pl.BlockSpec((None, tm, w), lambda bi, ti, j: (bi, ti, 0))
    full = lambda shape: pl.BlockSpec(shape, lambda bi, ti, j: (0, 0))
    gate_col = lambda rows: pl.BlockSpec((rows, FF_TILE), lambda bi, ti, j: (0, j))
    val_col = lambda rows: pl.BlockSpec((rows, FF_TILE), lambda bi, ti, j: (0, nj + j))
    return pl.pallas_call(
        functools.partial(_ffn_kernel, tm=tm),
        grid=(b, t // tm, nj),
        in_specs=[row(D_MODEL), row(MIX_HALF), row(MIX_HALF),
                  full((D_MODEL, D_MODEL)), full((1, D_MODEL)),
                  gate_col(D_MODEL), val_col(D_MODEL), gate_col(3), val_col(3), gate_col(1), val_col(1),
                  pl.BlockSpec((FF_TILE, D_MODEL), lambda bi, ti, j: (j, 0)),
                  full((1, D_MODEL))],
        out_specs=row(D_MODEL),
        out_shape=jax.ShapeDtypeStruct(x3.shape, F32),
        scratch_shapes=[pltpu.VMEM((tm, D_MODEL), F32), pltpu.VMEM((tm, D_MODEL), BF16),
                        pltpu.VMEM((tm, D_MODEL), F32),
                        pltpu.VMEM((tm + 8, FF_TILE), F32), pltpu.VMEM((tm + 8, FF_TILE), F32),
                        pltpu.VMEM((nj, 2, 8, FF_TILE), F32)],
        compiler_params=_params(("arbitrary", "arbitrary", "arbitrary")),
        name="outproj_convffn",
    )(x3, yr, yf, wo, nf, wu, wu, cw, cw, cb, cb, wd, nfin)


def _pack_w_in(w_in):
    fox = w_in[:, RWKV_COLS:]
    fl = fox[:, 4 * MIX_HALF:]
    pad = jnp.zeros((D_MODEL, FOX_Q_COL - RWKV_COLS - fl.shape[1]), w_in.dtype)
    return jnp.concatenate([w_in[:, :RWKV_COLS], fl, pad, fox[:, :4 * MIX_HALF]], axis=1).astype(BF16)


def _tile(n, pref):
    t = min(n, pref)
    assert n % t == 0, (n, t)
    return t


def kernel(x, norm_mix_w, w_in, rwkv_mu, rwkv_w0, rwkv_w2, rwkv_a0, rwkv_a2, rwkv_g2, rwkv_k_k, rwkv_k_a,
           rwkv_r_k, rwkv_lnx_w, rwkv_lnx_b, fox_f_bias, fox_q_norm_w, fox_k_norm_w, fox_o_norm_w, w_out,
           norm_ffn_w, ffn_w_up, ffn_conv_w, ffn_conv_b, ffn_w_down, norm_final_w):
    b, t, d = x.shape
    assert d == D_MODEL and norm_mix_w.shape[0] == 1 and t % CHUNK == 0
    m = b * t
    row = lambda a: a.reshape(1, -1).astype(F32)
    twice = lambda a: jnp.tile(a.reshape(1, HEAD_DIM), (1, 2)).astype(F32)

    p = _inproj(x.reshape(m, d), row(norm_mix_w[0]), _pack_w_in(w_in[0]), _tile(m, 1024), 1024)
    p3 = p.reshape(b, t, P_COLS)

    zeros = jnp.zeros((DECAY_LORA, MIX_HALF), F32)
    w2p = jnp.concatenate([rwkv_w2[0], zeros], axis=0).astype(BF16)
    a2p = jnp.concatenate([zeros, rwkv_a2[0]], axis=0).astype(BF16)
    vec_params = [row(a[0]) for a in (rwkv_w0, rwkv_a0, rwkv_k_k, rwkv_k_a, rwkv_r_k, rwkv_lnx_w, rwkv_lnx_b)]
    y_rwkv = _rwkv(p3, row(rwkv_mu[0]), vec_params, w2p, a2p, rwkv_g2[0].astype(BF16), _tile(t, 256))

    fb = jnp.zeros((1, LANES), F32).at[0, :N_HEADS].set(fox_f_bias[0])
    qa, ka, va = _fox_prep(p3, fb, twice(fox_q_norm_w[0]), twice(fox_k_norm_w[0]), _tile(t, 256))
    y_fox = _fox_attn(qa, ka, va, p3, twice(fox_o_norm_w[0]), _tile(t, 512))

    out = _ffn(x, y_rwkv, y_fox, w_out[0].astype(BF16), row(norm_ffn_w[0]), ffn_w_up[0].astype(BF16),
               ffn_conv_w[0].astype(F32), row(ffn_conv_b[0]), ffn_w_down[0].astype(BF16), row(norm_final_w),
               _tile(t, 512))
    return out.astype(x.dtype)
```

```python
import functools

import jax
import jax.numpy as jnp
from jax import lax
from jax.experimental import pallas as pl
from jax.experimental.pallas import tpu as pltpu

F32 = jnp.float32
BF16 = jnp.bfloat16

D_MODEL = 1024
HEAD_DIM = 64
N_HEADS = 8
MIX_HALF = N_HEADS * HEAD_DIM
DECAY_LORA = 64
AAA_LORA = 64
GATE_LORA = 128
RWKV_COLS = 3 * MIX_HALF + DECAY_LORA + AAA_LORA + GATE_LORA
D_FF = 2816
NORM_EPS = 1e-6
LNX_EPS = 64e-5

LANES = 128
PAIRS = MIX_HALF // LANES
CHUNK = 64
P_COLS = 4096
FL_BLOCK = RWKV_COLS // LANES
FOX_Q_COL = 2048
FF_TILE = 256
INPROJ_COLS = 1024
VMEM_LIMIT = 56 * 1024 * 1024
LOG2E = 1.4426950408889634

TILE_INPROJ = 512
TILE_RWKV = 256
TILE_FOX_PREP = 256
TILE_FOX_ATTN = 512
TILE_FFN = 512


def _params(sem):
    return pltpu.CompilerParams(dimension_semantics=sem, vmem_limit_bytes=VMEM_LIMIT)


def _split3(x):
    hi = x.astype(BF16)
    r1 = x - hi.astype(F32)
    mid = r1.astype(BF16)
    lo = (r1 - mid.astype(F32)).astype(BF16)
    return hi, mid, lo


def _dot(a, b, dims=None):
    if dims is None:
        dims = (((a.ndim - 1,), (0,)), ((), ()))
    return lax.dot_general(a, b, dims, preferred_element_type=F32)


_NT = (((1,), (1,)), ((), ()))

_BNN = (((2,), (1,)), ((0,), (0,)))
_BNT = (((2,), (2,)), ((0,), (0,)))
_BTN = (((1,), (1,)), ((0,), (0,)))


def _bmm(a, b, dims=_BNN):
    return lax.dot_general(a.astype(BF16), b.astype(BF16), dims, preferred_element_type=F32)


def _dot_exact_rhs(x, rhs_bf16):
    hi, mid, lo = _split3(x)
    return _dot(hi, rhs_bf16) + _dot(mid, rhs_bf16) + _dot(lo, rhs_bf16)


def _dot_exact_lhs(lhs_bf16, x):
    hi, mid, lo = _split3(x)
    return _dot(lhs_bf16, hi) + _dot(lhs_bf16, mid) + _dot(lhs_bf16, lo)


def _iota(shape, dim):
    return lax.broadcasted_iota(jnp.int32, shape, dim)


def _head_ones():
    r = _iota((LANES, LANES), 0) // HEAD_DIM
    c = _iota((LANES, LANES), 1) // HEAD_DIM
    return jnp.where(r == c, 1.0, 0.0).astype(BF16)


def _head_sum(x, ones_h):
    xb = x.astype(BF16)
    return jnp.concatenate([_dot(xb[:, i:i + LANES], ones_h) for i in range(0, x.shape[1], LANES)], axis=1)


def _block_tri(n):
    r = _iota((n, n), 0)
    c = _iota((n, n), 1)
    return jnp.where((r // CHUNK == c // CHUNK) & (c <= r), 1.0, 0.0).astype(BF16)


def _sigmoid(z):
    return 1.0 / (1.0 + jnp.exp(-z))


def _softplus(z):
    return jnp.maximum(z, 0.0) + jnp.log1p(jnp.exp(-jnp.abs(z)))


def _resident(shape):
    return pl.BlockSpec(shape, lambda *_: (0,) * len(shape), pipeline_mode=pl.Buffered(1))


def _inproj_kernel(x_ref, nw_ref, w_ref, o_ref, fl_ref):
    x = x_ref[...]
    ms = jnp.mean(x * x, axis=-1, keepdims=True)
    h = (x * lax.rsqrt(ms + NORM_EPS) * nw_ref[...]).astype(BF16)
    for c in range(0, P_COLS, INPROJ_COLS):
        o_ref[:, c:c + INPROJ_COLS] = _dot(h, w_ref[:, c:c + INPROJ_COLS]).astype(BF16)
    fl = FL_BLOCK * LANES
    fl_ref[...] = _dot(h, w_ref[:, fl:fl + LANES])


def _inproj(x2d, norm_w, w_packed, tm):
    m = x2d.shape[0]
    return pl.pallas_call(
        _inproj_kernel,
        grid=(m // tm,),
        in_specs=[pl.BlockSpec((tm, D_MODEL), lambda i: (i, 0)), _resident((1, D_MODEL)),
                  _resident((D_MODEL, P_COLS))],
        out_specs=[pl.BlockSpec((tm, P_COLS), lambda i: (i, 0)), pl.BlockSpec((tm, LANES), lambda i: (i, 0))],
        out_shape=[jax.ShapeDtypeStruct((m, P_COLS), BF16), jax.ShapeDtypeStruct((m, LANES), F32)],
        compiler_params=_params(("arbitrary",)),
        name="inproj",
    )(x2d, norm_w, w_packed)


def _rwkv_kernel(r_ref, k_ref, v_ref, wa_ref, gl_ref,
                 mur_ref, muk_ref, muv_ref, muwa_ref, mug_ref,
                 w0_ref, a0_ref, kk_ref, ka_ref, rk_ref, lnw_ref, lnb_ref,
                 w2_ref, a2_ref, g2_ref,
                 y_ref, pr_ref, pk_ref, pv_ref, pwa_ref, pg_ref, s_ref, *, tt):
    @pl.when(pl.program_id(1) == 0)
    def _():
        for ref in (pr_ref, pk_ref, pv_ref, pwa_ref, pg_ref, s_ref):
            ref[...] = jnp.zeros_like(ref)

    nc = tt // CHUNK

    def shifted(x_ref, prev_ref, mu_ref):
        x = x_ref[...].astype(F32)
        first_row = _iota(x.shape, 0) == 0
        prev = jnp.where(first_row, prev_ref[7:8, :], pltpu.roll(x, 1, axis=0))
        prev_ref[...] = x[tt - 8:, :]
        return x + (prev - x) * mu_ref[...]

    r = shifted(r_ref, pr_ref, mur_ref)
    k = shifted(k_ref, pk_ref, muk_ref)
    v = shifted(v_ref, pv_ref, muv_ref)
    wa = shifted(wa_ref, pwa_ref, muwa_ref)
    gl = shifted(gl_ref, pg_ref, mug_ref)

    ones_h = _head_ones()

    w_lora = _dot(jnp.tanh(wa).astype(BF16), w2_ref[...])
    a_lora = _dot(wa.astype(BF16), a2_ref[...])
    w = -_softplus(-(w0_ref[...] + w_lora)) - 0.5
    logd = -jnp.exp(w)
    a_sig = _sigmoid(a0_ref[...] + a_lora)
    g = _dot(_sigmoid(gl).astype(BF16), g2_ref[...])

    kk = k * kk_ref[...]
    kk = kk * lax.rsqrt(jnp.maximum(_head_sum(kk * kk, ones_h), 1e-24))
    k2 = k * (1.0 + (a_sig - 1.0) * ka_ref[...])
    bonus = _head_sum(r * k2 * rk_ref[...], ones_h) * v

    cs = _dot_exact_lhs(_block_tri(tt), logd)
    w_incl = jnp.exp(cs)
    w_inv = jnp.exp(-cs)
    a_t = (-kk) * jnp.exp(cs - logd)
    b_t = kk * a_sig * w_inv
    k_t = k2 * w_inv
    r_t = r * w_incl

    m0 = _iota((1, 1, LANES), 2) < HEAD_DIM

    def units(x):
        out = []
        for p in range(PAIRS):
            xp = x[:, p * LANES:(p + 1) * LANES].reshape(nc, CHUNK, LANES)
            out.append(jnp.concatenate([jnp.where(m0, xp, 0.0), jnp.where(m0, 0.0, xp)], axis=1))
        return jnp.concatenate(out, axis=0).astype(BF16)

    a_st, r_st, b_st, k_st, v_st = (units(x) for x in (a_t, r_t, b_t, k_t, v))
    wc = jnp.concatenate(
        [w_incl[:, p * LANES:(p + 1) * LANES].reshape(nc, CHUNK, LANES)[:, CHUNK - 1:, :] for p in range(PAIRS)],
        axis=0)

    ri = _iota((1, LANES, LANES), 1)
    ci = _iota((1, LANES, LANES), 2)
    same = (ri // CHUNK) == (ci // CHUNK)
    strict = same & (ci < ri)
    incl = same & (ci <= ri)
    eye = jnp.where(ri == ci, 1.0, 0.0).astype(F32)

    bk = jnp.concatenate([b_st, k_st], axis=1)
    sc = _bmm(jnp.concatenate([a_st, r_st], axis=1), bk, _BNT)
    l_ab = jnp.where(strict, sc[:, :LANES, :LANES], 0.0)
    l_ak = jnp.where(strict, sc[:, :LANES, LANES:], 0.0)
    m_rbk = jnp.concatenate([jnp.where(incl, sc[:, LANES:, :LANES], 0.0),
                             jnp.where(incl, sc[:, LANES:, LANES:], 0.0)], axis=2).astype(BF16)

    t_inv = eye + l_ab
    lp = l_ab.astype(BF16)
    lp = _bmm(lp, lp).astype(BF16)
    for _ in range(CHUNK.bit_length() - 3):
        both = _bmm(jnp.concatenate([lp, t_inv.astype(BF16)], axis=1), lp)
        t_inv = t_inv + both[:, LANES:]
        lp = both[:, :LANES].astype(BF16)
    t_inv = t_inv + _bmm(t_inv, lp)

    lakv = _bmm(l_ak, v_st)
    au = _bmm(t_inv, jnp.concatenate([a_st, lakv.astype(BF16)], axis=2))
    au_b = au.astype(BF16)
    zv = jnp.concatenate([jnp.zeros_like(v_st), v_st], axis=2)
    ry = _bmm(m_rbk, jnp.concatenate([au_b, zv], axis=1))
    r2_st = r_st.astype(F32) + ry[:, :, :LANES]
    y0_st = ry[:, :, LANES:]
    g_mat = (eye + _bmm(au_b[:, :, :LANES], b_st, _BTN)) * wc
    h_mat = _bmm(jnp.concatenate([au_b[:, :, LANES:], v_st], axis=1), bk, _BTN) * wc

    def chunk(x, c):
        return x.reshape((PAIRS, nc) + x.shape[1:])[:, c]

    s = s_ref[...]
    ys = []
    for c in range(nc):
        y_st = _bmm(chunk(r2_st, c), s, _BNT) + chunk(y0_st, c)
        s = _bmm(s, chunk(g_mat, c)) + chunk(h_mat, c)
        ys.append(y_st[:, :CHUNK] + y_st[:, CHUNK:])
    s_ref[...] = s
    y = jnp.concatenate([jnp.concatenate([yc[p] for yc in ys], axis=0) for p in range(PAIRS)], axis=1)

    mean = _head_sum(y, ones_h) * (1.0 / HEAD_DIM)
    yc = y - mean
    var = _head_sum(yc * yc, ones_h) * (1.0 / HEAD_DIM)
    yn = yc * lax.rsqrt(var + LNX_EPS) * lnw_ref[...] + lnb_ref[...]
    y_ref[...] = ((yn + bonus) * g).astype(y_ref.dtype)


def _rwkv(p3, mu, vec_params, w2p, a2p, g2, tt):
    b, t, _ = p3.shape
    nblk = MIX_HALF // LANES
    wide = lambda cb: pl.BlockSpec((None, tt, MIX_HALF), lambda bi, ti, cb=cb: (bi, ti, cb))
    narrow = lambda cb: pl.BlockSpec((None, tt, LANES), lambda bi, ti, cb=cb: (bi, ti, cb))
    mu_wide = lambda cb: pl.BlockSpec((1, MIX_HALF), lambda bi, ti, cb=cb: (0, cb))
    mu_narrow = lambda cb: pl.BlockSpec((1, LANES), lambda bi, ti, cb=cb: (0, cb))
    vec = pl.BlockSpec((1, MIX_HALF), lambda bi, ti: (0, 0))
    lora = pl.BlockSpec((LANES, MIX_HALF), lambda bi, ti: (0, 0))
    return pl.pallas_call(
        functools.partial(_rwkv_kernel, tt=tt),
        grid=(b, t // tt),
        in_specs=[wide(0), wide(1), wide(2), narrow(3 * nblk), narrow(3 * nblk + 1),
                  mu_wide(0), mu_wide(1), mu_wide(2), mu_narrow(3 * nblk), mu_narrow(3 * nblk + 1)]
        + [vec] * len(vec_params) + [lora] * 3,
        out_specs=pl.BlockSpec((None, tt, MIX_HALF), lambda bi, ti: (bi, ti, 0)),
        out_shape=jax.ShapeDtypeStruct((b, t, MIX_HALF), BF16),
        scratch_shapes=[pltpu.VMEM((8, MIX_HALF), F32)] * 3 + [pltpu.VMEM((8, LANES), F32)] * 2
        + [pltpu.VMEM((PAIRS, LANES, LANES), F32)],
        compiler_params=_params(("arbitrary", "arbitrary")),
        name="rwkv7",
    )(*([p3] * 5), *([mu] * 5), *vec_params, w2p, a2p, g2)


def _fox_prep_kernel(q_ref, k_ref, v_ref, fl_ref, fb_ref, qw_ref, kw_ref,
                     qa_ref, ka_ref, va_ref, carry_ref, *, tt):
    @pl.when(pl.program_id(1) == 0)
    def _():
        carry_ref[...] = jnp.zeros_like(carry_ref)

    lane = _iota((1, LANES), 1)
    ones_h = _head_ones()
    r_t = _iota((tt, tt), 0)
    c_t = _iota((tt, tt), 1)
    tri = jnp.where(c_t <= r_t, 1.0, 0.0).astype(BF16)

    logf = -_softplus(-(fl_ref[...] + fb_ref[...]))
    c = _dot_exact_lhs(tri, logf) + carry_ref[0:1, :]
    carry_ref[...] = jnp.broadcast_to(c[tt - 1:tt, :], carry_ref.shape)
    c_parts = _split3(c * LOG2E)

    sel_r = _iota((LANES, LANES), 0)
    for pb in range(PAIRS):
        sl = slice(pb * LANES, (pb + 1) * LANES)
        q = q_ref[:, sl].astype(F32)
        k = k_ref[:, sl].astype(F32)
        v = v_ref[:, sl]
        qn = q * lax.rsqrt(_head_sum(q * q, ones_h) * (1.0 / HEAD_DIM) + NORM_EPS) * qw_ref[...]
        kn = k * lax.rsqrt(_head_sum(k * k, ones_h) * (1.0 / HEAD_DIM) + NORM_EPS) * kw_ref[...]
        qn = qn * (HEAD_DIM ** -0.5 * LOG2E)
        for half in range(2):
            h = 2 * pb + half
            own = (lane // HEAD_DIM) == half
            a0 = HEAD_DIM * (1 - half)
            sel = jnp.where(sel_r == h, 1.0, 0.0).astype(BF16)
            cb = [_dot(part, sel) for part in c_parts]
            qa = jnp.where(own, qn, 0.0)
            ka = jnp.where(own, kn, 0.0)
            for i in range(3):
                qa = jnp.where(lane == a0 + i, cb[i], qa)
                ka = jnp.where(lane == a0 + 3 + i, -cb[i], ka)
            qa = jnp.where((lane >= a0 + 3) & (lane < a0 + 6), 1.0, qa)
            ka = jnp.where((lane >= a0) & (lane < a0 + 3), 1.0, ka)
            va = jnp.where(own, v, jnp.where(lane == a0, 1.0, 0.0).astype(BF16))
            qa_ref[h] = qa.astype(BF16)
            ka_ref[h] = ka.astype(BF16)
            va_ref[h] = va


def _fox_prep(p3, fl3, f_bias, qw, kw, tt):
    b, t, _ = p3.shape

    def tok(cb):
        return pl.BlockSpec((None, tt, MIX_HALF), lambda bi, ti, cb=cb: (bi, ti, cb))

    vec = pl.BlockSpec((1, LANES), lambda bi, ti: (0, 0))
    out_spec = pl.BlockSpec((None, N_HEADS, tt, LANES), lambda bi, ti: (bi, 0, ti, 0))
    out_shape = jax.ShapeDtypeStruct((b, N_HEADS, t, LANES), BF16)
    qcb = FOX_Q_COL // MIX_HALF
    return pl.pallas_call(
        functools.partial(_fox_prep_kernel, tt=tt),
        grid=(b, t // tt),
        in_specs=[tok(qcb), tok(qcb + 1), tok(qcb + 2),
                  pl.BlockSpec((None, tt, LANES), lambda bi, ti: (bi, ti, 0)),
                  vec, vec, vec],
        out_specs=[out_spec] * 3,
        out_shape=[out_shape] * 3,
        scratch_shapes=[pltpu.VMEM((8, LANES), F32)],
        compiler_params=_params(("arbitrary", "arbitrary")),
        name="fox_prep",
    )(p3, p3, p3, fl3, f_bias, qw, kw)


def _fox_attn_kernel(qa_ref, ka_ref, va_ref, og_ref, ow_ref, y_ref, m_ref, acc_ref, *, tq):
    i = pl.program_id(2)
    lane = _iota((1, LANES), 1)
    causal = _iota((tq, tq), 1) <= _iota((tq, tq), 0)
    m_ref[...] = jnp.full(m_ref.shape, -jnp.inf, F32)
    acc_ref[...] = jnp.zeros_like(acc_ref)
    q = [qa_ref[0], qa_ref[1]]

    def step(j, masked):
        start = pl.multiple_of(j * tq, tq)
        s = [_dot(q[h], ka_ref[h, pl.ds(start, tq), :], _NT) for h in range(2)]
        if masked:
            s = [jnp.where(causal, x, -jnp.inf) for x in s]
        m_old = [m_ref[h] for h in range(2)]
        m_new = [jnp.maximum(m_old[h], jnp.max(s[h], axis=1, keepdims=True)) for h in range(2)]
        p = [jnp.concatenate([jnp.exp2(s[h][:, c:c + LANES] - m_new[h]) for c in range(0, tq, LANES)],
                             axis=1).astype(BF16) for h in range(2)]
        pv = [_dot(p[h], va_ref[h, pl.ds(start, tq), :]) for h in range(2)]
        for h in range(2):
            acc_ref[h] = jnp.exp2(m_old[h] - m_new[h]) * acc_ref[h] + pv[h]
            m_ref[h] = m_new[h]

    def body(j, carry):
        step(j, False)
        return carry

    lax.fori_loop(0, i, body, 0)
    step(i, True)

    halves = []
    for h in range(2):
        acc = acc_ref[h]
        denom = jnp.sum(jnp.where(lane == HEAD_DIM * (1 - h), acc, 0.0), axis=1, keepdims=True)
        halves.append(acc / denom)
    o = jnp.where(lane < HEAD_DIM, halves[0], halves[1])
    ms = _head_sum(o * o, _head_ones()) * (1.0 / HEAD_DIM)
    gate = _sigmoid(og_ref[...].astype(F32))
    y_ref[...] = (o * lax.rsqrt(ms + NORM_EPS) * ow_ref[...] * gate).astype(y_ref.dtype)


def _fox_attn(qa, ka, va, p3, ow, tq):
    b, _, t, _ = qa.shape
    og_cb = (FOX_Q_COL + 3 * MIX_HALF) // LANES
    kv_spec = pl.BlockSpec((None, 2, t, LANES), lambda bi, hp, qi: (bi, hp, 0, 0))
    return pl.pallas_call(
        functools.partial(_fox_attn_kernel, tq=tq),
        grid=(b, PAIRS, t // tq),
        in_specs=[pl.BlockSpec((None, 2, tq, LANES), lambda bi, hp, qi: (bi, hp, qi, 0)),
                  kv_spec, kv_spec,
                  pl.BlockSpec((None, tq, LANES), lambda bi, hp, qi: (bi, qi, og_cb + hp)),
                  pl.BlockSpec((1, LANES), lambda bi, hp, qi: (0, 0))],
        out_specs=pl.BlockSpec((None, tq, LANES), lambda bi, hp, qi: (bi, qi, hp)),
        out_shape=jax.ShapeDtypeStruct((b, t, MIX_HALF), BF16),
        scratch_shapes=[pltpu.VMEM((2, tq, LANES), F32), pltpu.VMEM((2, tq, LANES), F32)],
        compiler_params=_params(("arbitrary", "arbitrary", "arbitrary")),
        name="fox_attn",
    )(qa, ka, va, p3, ow)


def _ffn_kernel(x_ref, yr_ref, yf_ref, wo_ref, nf_ref, wu_ref, cw_ref, cb_ref, wd_ref, nfin_ref, o_ref,
                x2_ref, hid_ref, carry_ref, *, tm):
    @pl.when(pl.program_id(1) == 0)
    def _():
        carry_ref[...] = jnp.zeros_like(carry_ref)

    x2 = (x_ref[...] + _dot(yr_ref[...], wo_ref[:MIX_HALF, :]) + _dot(yf_ref[...], wo_ref[MIX_HALF:, :]))
    x2_ref[...] = x2
    ms = jnp.mean(x2 * x2, axis=-1, keepdims=True)
    h2 = (x2 * lax.rsqrt(ms + NORM_EPS) * nf_ref[...]).astype(BF16)

    width = 2 * FF_TILE

    def up(c):
        return _dot(h2, wu_ref[:, c * width:(c + 1) * width])

    def conv_glu(c, u):
        cols = slice(c * width, (c + 1) * width)
        ext = jnp.concatenate([carry_ref[:, cols], u], axis=0)
        carry_ref[:, cols] = u[tm - 8:, :]
        u1 = pltpu.roll(ext, 1, axis=0)[8:, :]
        u2 = pltpu.roll(ext, 2, axis=0)[8:, :]
        cw = cw_ref[:, cols]
        y = cw[0:1, :] * u2 + cw[1:2, :] * u1 + cw[2:3, :] * u + cb_ref[:, cols]
        gate = y[:, :FF_TILE]
        hid_ref[:, c * FF_TILE:(c + 1) * FF_TILE] = (gate * _sigmoid(gate) * y[:, FF_TILE:]).astype(BF16)

    n = D_FF // FF_TILE
    u_prev = up(0)
    for c in range(1, n):
        u_next = up(c)
        conv_glu(c - 1, u_prev)
        u_prev = u_next
    conv_glu(n - 1, u_prev)

    xo = x2_ref[...] + _dot(hid_ref[...], wd_ref[...])
    ms = jnp.mean(xo * xo, axis=-1, keepdims=True)
    o_ref[...] = xo * lax.rsqrt(ms + NORM_EPS) * nfin_ref[...]


def _ffn(x3, yr, yf, wo, nf, wu, cw, cb, wd, nfin, tm):
    b, t, _ = x3.shape
    row = lambda w: pl.BlockSpec((None, tm, w), lambda bi, ti: (bi, ti, 0))
    return pl.pallas_call(
        functools.partial(_ffn_kernel, tm=tm),
        grid=(b, t // tm),
        in_specs=[row(D_MODEL), row(MIX_HALF), row(MIX_HALF)]
        + [_resident(a.shape) for a in (wo, nf, wu, cw, cb, wd, nfin)],
        out_specs=row(D_MODEL),
        out_shape=jax.ShapeDtypeStruct(x3.shape, F32),
        scratch_shapes=[pltpu.VMEM((tm, D_MODEL), F32), pltpu.VMEM((tm, D_FF), BF16),
                        pltpu.VMEM((8, 2 * D_FF), F32)],
        compiler_params=_params(("arbitrary", "arbitrary")),
        name="outproj_convffn",
    )(x3, yr, yf, wo, nf, wu, cw, cb, wd, nfin)


def _interleave_ff(a):
    lead = a.shape[:-1]
    return a.reshape(lead + (2, D_FF // FF_TILE, FF_TILE)).swapaxes(-3, -2).reshape(lead + (2 * D_FF,))


def _pack_w_in(w_in):
    fox = w_in[:, RWKV_COLS:]
    fl = fox[:, 4 * MIX_HALF:]
    pad = jnp.zeros((D_MODEL, FOX_Q_COL - RWKV_COLS - fl.shape[1]), w_in.dtype)
    return jnp.concatenate([w_in[:, :RWKV_COLS], fl, pad, fox[:, :4 * MIX_HALF]], axis=1).astype(BF16)


def _tile(n, pref):
    t = min(n, pref)
    assert n % t == 0, (n, t)
    return t


def kernel(x, norm_mix_w, w_in, rwkv_mu, rwkv_w0, rwkv_w2, rwkv_a0, rwkv_a2, rwkv_g2, rwkv_k_k, rwkv_k_a,
           rwkv_r_k, rwkv_lnx_w, rwkv_lnx_b, fox_f_bias, fox_q_norm_w, fox_k_norm_w, fox_o_norm_w, w_out,
           norm_ffn_w, ffn_w_up, ffn_conv_w, ffn_conv_b, ffn_w_down, norm_final_w):
    b, t, d = x.shape
    assert d == D_MODEL and norm_mix_w.shape[0] == 1 and t % CHUNK == 0
    m = b * t
    row = lambda a: a.reshape(1, -1).astype(F32)
    twice = lambda a: jnp.tile(a.reshape(1, HEAD_DIM), (1, 2)).astype(F32)

    p, fl = _inproj(x.reshape(m, d), row(norm_mix_w[0]), _pack_w_in(w_in[0]), _tile(m, TILE_INPROJ))
    p3 = p.reshape(b, t, P_COLS)

    zeros = jnp.zeros((DECAY_LORA, MIX_HALF), F32)
    w2p = jnp.concatenate([rwkv_w2[0], zeros], axis=0).astype(BF16)
    a2p = jnp.concatenate([zeros, rwkv_a2[0]], axis=0).astype(BF16)
    vec_params = [row(a[0]) for a in (rwkv_w0, rwkv_a0, rwkv_k_k, rwkv_k_a, rwkv_r_k, rwkv_lnx_w, rwkv_lnx_b)]
    y_rwkv = _rwkv(p3, row(rwkv_mu[0]), vec_params, w2p, a2p, rwkv_g2[0].astype(BF16), _tile(t, TILE_RWKV))

    fb = jnp.zeros((1, LANES), F32).at[0, :N_HEADS].set(fox_f_bias[0])
    qa, ka, va = _fox_prep(p3, fl.reshape(b, t, LANES), fb, twice(fox_q_norm_w[0]), twice(fox_k_norm_w[0]),
                           _tile(t, TILE_FOX_PREP))
    y_fox = _fox_attn(qa, ka, va, p3, twice(fox_o_norm_w[0]), _tile(t, TILE_FOX_ATTN))

    out = _ffn(x, y_rwkv, y_fox, w_out[0].astype(BF16), row(norm_ffn_w[0]),
               _interleave_ff(ffn_w_up[0]).astype(BF16), _interleave_ff(ffn_conv_w[0]).astype(F32),
               _interleave_ff(row(ffn_conv_b[0])), ffn_w_down[0].astype(BF16), row(norm_final_w),
               _tile(t, TILE_FFN))
    return out.astype(x.dtype)
```

```python
import functools

import jax
import jax.numpy as jnp
import numpy as np
from jax import lax
from jax.experimental import pallas as pl
from jax.experimental.pallas import tpu as pltpu

F32 = jnp.float32
BF16 = jnp.bfloat16

D_MODEL = 1024
HEAD_DIM = 64
N_HEADS = 8
MIX_HALF = N_HEADS * HEAD_DIM
DECAY_LORA = 64
AAA_LORA = 64
GATE_LORA = 128
RWKV_COLS = 3 * MIX_HALF + DECAY_LORA + AAA_LORA + GATE_LORA
D_FF = 2816
NORM_EPS = 1e-6
LNX_EPS = 64e-5

LANES = 128
PAIRS = MIX_HALF // LANES
CHUNK = 64
FL_BLOCK = RWKV_COLS // LANES
FOX_Q_COL = 2048
FF_CHUNK = 512
INPROJ_COLS = 1024
CUMSUM_ROWS = 256
VMEM_LIMIT = 56 * 1024 * 1024
LOG2E = 1.4426950408889634

TILE_INPROJ = 512
TILE_RWKV = 256
TILE_FOX_ATTN = 512
TILE_FFN = 512


def _params(sem):
    return pltpu.CompilerParams(dimension_semantics=sem, vmem_limit_bytes=VMEM_LIMIT)


def _split3(x):
    hi = x.astype(BF16)
    r1 = x - hi.astype(F32)
    mid = r1.astype(BF16)
    lo = (r1 - mid.astype(F32)).astype(BF16)
    return hi, mid, lo


def _dot(a, b, dims=None):
    if dims is None:
        dims = (((a.ndim - 1,), (0,)), ((), ()))
    return lax.dot_general(a, b, dims, preferred_element_type=F32)


_NT = (((1,), (1,)), ((), ()))

_BNN = (((2,), (1,)), ((0,), (0,)))
_BNT = (((2,), (2,)), ((0,), (0,)))
_BTN = (((1,), (1,)), ((0,), (0,)))


def _bmm(a, b, dims=_BNN):
    return lax.dot_general(a.astype(BF16), b.astype(BF16), dims, preferred_element_type=F32)


def _dot_exact_lhs(lhs_bf16, x):
    hi, mid, lo = _split3(x)
    return _dot(lhs_bf16, hi) + _dot(lhs_bf16, mid) + _dot(lhs_bf16, lo)


def _iota(shape, dim):
    return lax.broadcasted_iota(jnp.int32, shape, dim)


def _head_ones():
    r = _iota((LANES, LANES), 0) // HEAD_DIM
    c = _iota((LANES, LANES), 1) // HEAD_DIM
    return jnp.where(r == c, 1.0, 0.0).astype(BF16)


def _head_sum(x, ones_h):
    xb = x.astype(BF16)
    return jnp.concatenate([_dot(xb[:, i:i + LANES], ones_h) for i in range(0, x.shape[1], LANES)], axis=1)


def _block_tri(n):
    r = _iota((n, n), 0)
    c = _iota((n, n), 1)
    return jnp.where((r // CHUNK == c // CHUNK) & (c <= r), 1.0, 0.0).astype(BF16)


def _sigmoid(z):
    return 1.0 / (1.0 + jnp.exp(-z))


def _softplus(z):
    return jnp.maximum(z, 0.0) + jnp.log1p(jnp.exp(-jnp.abs(z)))


def _resident(shape):
    return pl.BlockSpec(shape, lambda *_: (0,) * len(shape), pipeline_mode=pl.Buffered(1))


def _aug_tables():
    sel = np.zeros((LANES, N_HEADS * LANES), np.float32)
    for h in range(N_HEADS):
        a0 = h * LANES + HEAD_DIM * (1 - h % 2)
        for i in range(3):
            sel[i * N_HEADS + h, a0 + i] = 1.0
            sel[i * N_HEADS + h, a0 + 3 + i] = -1.0
    return jnp.asarray(sel, BF16)


def _inproj_kernel(x_ref, nw_ref, w_ref, fb_ref, qw_ref, kw_ref, sel_ref,
                   pr_ref, og_ref, qa_ref, ka_ref, va_ref, carry_ref, *, tm):
    @pl.when(pl.program_id(1) == 0)
    def _():
        carry_ref[...] = jnp.zeros_like(carry_ref)

    x = x_ref[...]
    ms = jnp.mean(x * x, axis=-1, keepdims=True)
    h = (x * lax.rsqrt(ms + NORM_EPS) * nw_ref[...]).astype(BF16)
    for lo in range(0, RWKV_COLS, INPROJ_COLS):
        hi = min(lo + INPROJ_COLS, RWKV_COLS)
        pr_ref[:, lo:hi] = _dot(h, w_ref[:, lo:hi]).astype(BF16)
    og = FOX_Q_COL + 3 * MIX_HALF
    og_ref[...] = _dot(h, w_ref[:, og:og + MIX_HALF]).astype(BF16)

    fl = FL_BLOCK * LANES
    logf = -_softplus(-(_dot(h, w_ref[:, fl:fl + LANES]) + fb_ref[...]))
    blk = min(tm, CUMSUM_ROWS)
    tri = jnp.where(_iota((blk, blk), 1) <= _iota((blk, blk), 0), 1.0, 0.0).astype(BF16)
    run = carry_ref[0:1, :]
    c_blocks = []
    for r0 in range(0, tm, blk):
        cb = _dot_exact_lhs(tri, logf[r0:r0 + blk, :]) + run
        run = cb[blk - 1:blk, :]
        c_blocks.append(cb)
    carry_ref[...] = jnp.broadcast_to(run, carry_ref.shape)
    c = jnp.concatenate(c_blocks, axis=0)
    lane = _iota((1, LANES), 1)
    c_hi, c_mid, c_lo = (p.astype(F32) for p in _split3(c * LOG2E))
    packed = jnp.where(lane < N_HEADS, c_hi, jnp.where(lane < 2 * N_HEADS, pltpu.roll(c_mid, N_HEADS, axis=1),
                                                       pltpu.roll(c_lo, 2 * N_HEADS, axis=1)))
    aug = _dot(packed.astype(BF16), sel_ref[...])

    r2 = _iota((2 * LANES, 2 * LANES), 0) // HEAD_DIM
    c2 = _iota((2 * LANES, 2 * LANES), 1) // HEAD_DIM
    ones_qk = jnp.where(r2 == c2, 1.0, 0.0).astype(BF16)
    qkv = _dot(h, w_ref[:, FOX_Q_COL:FOX_Q_COL + 3 * MIX_HALF])
    for pb in range(PAIRS):
        q, k, v = (qkv[:, j * MIX_HALF + pb * LANES:j * MIX_HALF + (pb + 1) * LANES] for j in range(3))
        ss = _dot(jnp.concatenate([q * q, k * k], axis=1).astype(BF16), ones_qk) * (1.0 / HEAD_DIM)
        qn = q * lax.rsqrt(ss[:, :LANES] + NORM_EPS) * (qw_ref[...] * (HEAD_DIM ** -0.5 * LOG2E))
        kn = k * lax.rsqrt(ss[:, LANES:] + NORM_EPS) * kw_ref[...]
        for half in range(2):
            hd = 2 * pb + half
            own = (lane // HEAD_DIM) == half
            a0 = HEAD_DIM * (1 - half)
            z = aug[:, hd * LANES:(hd + 1) * LANES]
            c_q = (lane >= a0) & (lane < a0 + 3)
            c_k = (lane >= a0 + 3) & (lane < a0 + 6)
            qa = jnp.where(own, qn, jnp.where(c_q, z, jnp.where(c_k, 1.0, 0.0)))
            ka = jnp.where(own, kn, jnp.where(c_k, z, jnp.where(c_q, 1.0, 0.0)))
            va = jnp.where(own, v, jnp.where(lane == a0, 1.0, 0.0))
            qa_ref[hd] = qa.astype(BF16)
            ka_ref[hd] = ka.astype(BF16)
            va_ref[hd] = va.astype(BF16)


def _inproj(x3, norm_w, w_packed, f_bias, qw, kw, tm):
    b, t, _ = x3.shape
    row = lambda w: pl.BlockSpec((None, tm, w), lambda bi, ti: (bi, ti, 0))
    head_spec = pl.BlockSpec((None, N_HEADS, tm, LANES), lambda bi, ti: (bi, 0, ti, 0))
    head_shape = jax.ShapeDtypeStruct((b, N_HEADS, t, LANES), BF16)
    consts = (norm_w, w_packed, f_bias, qw, kw, _aug_tables())
    return pl.pallas_call(
        functools.partial(_inproj_kernel, tm=tm),
        grid=(b, t // tm),
        in_specs=[row(D_MODEL)] + [_resident(a.shape) for a in consts],
        out_specs=[row(RWKV_COLS), row(MIX_HALF), head_spec, head_spec, head_spec],
        out_shape=[jax.ShapeDtypeStruct((b, t, RWKV_COLS), BF16), jax.ShapeDtypeStruct((b, t, MIX_HALF), BF16),
                   head_shape, head_shape, head_shape],
        scratch_shapes=[pltpu.VMEM((8, LANES), F32)],
        compiler_params=_params(("arbitrary", "arbitrary")),
        name="inproj_foxprep",
    )(x3, *consts)


def _rwkv_kernel(r_ref, k_ref, v_ref, wa_ref, gl_ref,
                 mur_ref, muk_ref, muv_ref, muwa_ref, mug_ref,
                 w0_ref, a0_ref, kk_ref, ka_ref, rk_ref, lnw_ref, lnb_ref,
                 w2_ref, a2_ref, g2_ref,
                 y_ref, pr_ref, pk_ref, pv_ref, pwa_ref, pg_ref, s_ref, *, tt):
    @pl.when(pl.program_id(1) == 0)
    def _():
        for ref in (pr_ref, pk_ref, pv_ref, pwa_ref, pg_ref, s_ref):
            ref[...] = jnp.zeros_like(ref)

    nc = tt // CHUNK

    def shifted(x_ref, prev_ref, mu_ref):
        x = x_ref[...].astype(F32)
        first_row = _iota(x.shape, 0) == 0
        prev = jnp.where(first_row, prev_ref[7:8, :], pltpu.roll(x, 1, axis=0))
        prev_ref[...] = x[tt - 8:, :]
        return x + (prev - x) * mu_ref[...]

    r = shifted(r_ref, pr_ref, mur_ref)
    k = shifted(k_ref, pk_ref, muk_ref)
    v = shifted(v_ref, pv_ref, muv_ref)
    wa = shifted(wa_ref, pwa_ref, muwa_ref)
    gl = shifted(gl_ref, pg_ref, mug_ref)

    ones_h = _head_ones()

    w_lora = _dot(jnp.tanh(wa).astype(BF16), w2_ref[...])
    a_lora = _dot(wa.astype(BF16), a2_ref[...])
    w = -_softplus(-(w0_ref[...] + w_lora)) - 0.5
    logd = -jnp.exp(w)
    a_sig = _sigmoid(a0_ref[...] + a_lora)
    g = _dot(_sigmoid(gl).astype(BF16), g2_ref[...])

    kk = k * kk_ref[...]
    kk = kk * lax.rsqrt(jnp.maximum(_head_sum(kk * kk, ones_h), 1e-24))
    k2 = k * (1.0 + (a_sig - 1.0) * ka_ref[...])
    bonus = _head_sum(r * k2 * rk_ref[...], ones_h) * v

    cs = _dot_exact_lhs(_block_tri(tt), logd)
    w_incl = jnp.exp(cs)
    w_inv = jnp.exp(-cs)
    a_t = (-kk) * jnp.exp(cs - logd)
    b_t = kk * a_sig * w_inv
    k_t = k2 * w_inv
    r_t = r * w_incl

    m0 = _iota((1, 1, LANES), 2) < HEAD_DIM

    def units(x):
        out = []
        for p in range(PAIRS):
            xp = x[:, p * LANES:(p + 1) * LANES].reshape(nc, CHUNK, LANES)
            out.append(jnp.concatenate([jnp.where(m0, xp, 0.0), jnp.where(m0, 0.0, xp)], axis=1))
        return jnp.concatenate(out, axis=0).astype(BF16)

    a_st, r_st, b_st, k_st, v_st = (units(x) for x in (a_t, r_t, b_t, k_t, v))
    wc = jnp.concatenate(
        [w_incl[:, p * LANES:(p + 1) * LANES].reshape(nc, CHUNK, LANES)[:, CHUNK - 1:, :] for p in range(PAIRS)],
        axis=0)

    ri = _iota((1, LANES, LANES), 1)
    ci = _iota((1, LANES, LANES), 2)
    same = (ri // CHUNK) == (ci // CHUNK)
    strict = same & (ci < ri)
    incl = same & (ci <= ri)
    eye = jnp.where(ri == ci, 1.0, 0.0).astype(F32)

    bk = jnp.concatenate([b_st, k_st], axis=1)
    sc = _bmm(jnp.concatenate([a_st, r_st], axis=1), bk, _BNT)
    l_ab = jnp.where(strict, sc[:, :LANES, :LANES], 0.0)
    l_ak = jnp.where(strict, sc[:, :LANES, LANES:], 0.0)
    m_rbk = jnp.concatenate([jnp.where(incl, sc[:, LANES:, :LANES], 0.0),
                             jnp.where(incl, sc[:, LANES:, LANES:], 0.0)], axis=2).astype(BF16)

    t_inv = eye + l_ab
    lp = l_ab.astype(BF16)
    lp = _bmm(lp, lp).astype(BF16)
    for _ in range(CHUNK.bit_length() - 3):
        both = _bmm(jnp.concatenate([lp, t_inv.astype(BF16)], axis=1), lp)
        t_inv = t_inv + both[:, LANES:]
        lp = both[:, :LANES].astype(BF16)
    t_inv = t_inv + _bmm(t_inv, lp)

    lakv = _bmm(l_ak, v_st)
    au = _bmm(t_inv, jnp.concatenate([a_st, lakv.astype(BF16)], axis=2))
    au_b = au.astype(BF16)
    zv = jnp.concatenate([jnp.zeros_like(v_st), v_st], axis=2)
    ry = _bmm(m_rbk, jnp.concatenate([au_b, zv], axis=1))
    r2_st = r_st.astype(F32) + ry[:, :, :LANES]
    y0_st = ry[:, :, LANES:]
    g_mat = (eye + _bmm(au_b[:, :, :LANES], b_st, _BTN)) * wc
    h_mat = _bmm(jnp.concatenate([au_b[:, :, LANES:], v_st], axis=1), bk, _BTN) * wc

    def chunk(x, c):
        return x.reshape((PAIRS, nc) + x.shape[1:])[:, c]

    s = s_ref[...]
    ys = []
    for c in range(nc):
        y_st = _bmm(chunk(r2_st, c), s, _BNT) + chunk(y0_st, c)
        s = _bmm(s, chunk(g_mat, c)) + chunk(h_mat, c)
        ys.append(y_st[:, :CHUNK] + y_st[:, CHUNK:])
    s_ref[...] = s
    y = jnp.concatenate([jnp.concatenate([yc[p] for yc in ys], axis=0) for p in range(PAIRS)], axis=1)

    mean = _head_sum(y, ones_h) * (1.0 / HEAD_DIM)
    yc = y - mean
    var = _head_sum(yc * yc, ones_h) * (1.0 / HEAD_DIM)
    yn = yc * lax.rsqrt(var + LNX_EPS) * lnw_ref[...] + lnb_ref[...]
    y_ref[...] = ((yn + bonus) * g).astype(y_ref.dtype)


def _rwkv(p3, mu, vec_params, w2p, a2p, g2, tt):
    b, t, _ = p3.shape
    nblk = MIX_HALF // LANES
    wide = lambda cb: pl.BlockSpec((None, tt, MIX_HALF), lambda bi, ti, cb=cb: (bi, ti, cb))
    narrow = lambda cb: pl.BlockSpec((None, tt, LANES), lambda bi, ti, cb=cb: (bi, ti, cb))
    mu_wide = lambda cb: pl.BlockSpec((1, MIX_HALF), lambda bi, ti, cb=cb: (0, cb))
    mu_narrow = lambda cb: pl.BlockSpec((1, LANES), lambda bi, ti, cb=cb: (0, cb))
    vec = pl.BlockSpec((1, MIX_HALF), lambda bi, ti: (0, 0))
    lora = pl.BlockSpec((LANES, MIX_HALF), lambda bi, ti: (0, 0))
    return pl.pallas_call(
        functools.partial(_rwkv_kernel, tt=tt),
        grid=(b, t // tt),
        in_specs=[wide(0), wide(1), wide(2), narrow(3 * nblk), narrow(3 * nblk + 1),
                  mu_wide(0), mu_wide(1), mu_wide(2), mu_narrow(3 * nblk), mu_narrow(3 * nblk + 1)]
        + [vec] * len(vec_params) + [lora] * 3,
        out_specs=pl.BlockSpec((None, tt, MIX_HALF), lambda bi, ti: (bi, ti, 0)),
        out_shape=jax.ShapeDtypeStruct((b, t, MIX_HALF), BF16),
        scratch_shapes=[pltpu.VMEM((8, MIX_HALF), F32)] * 3 + [pltpu.VMEM((8, LANES), F32)] * 2
        + [pltpu.VMEM((PAIRS, LANES, LANES), F32)],
        compiler_params=_params(("arbitrary", "arbitrary")),
        name="rwkv7",
    )(*([p3] * 5), *([mu] * 5), *vec_params, w2p, a2p, g2)


def _fox_attn_kernel(qa_ref, ka_ref, va_ref, og_ref, ow_ref, y_ref, m_ref, acc_ref, *, tq):
    i = pl.program_id(2)
    lane = _iota((1, LANES), 1)
    causal = _iota((tq, tq), 1) <= _iota((tq, tq), 0)
    m_ref[...] = jnp.full(m_ref.shape, -jnp.inf, F32)
    acc_ref[...] = jnp.zeros_like(acc_ref)
    q = [qa_ref[0], qa_ref[1]]

    def step(j, masked):
        start = pl.multiple_of(j * tq, tq)
        s = [_dot(q[h], ka_ref[h, pl.ds(start, tq), :], _NT) for h in range(2)]
        if masked:
            s = [jnp.where(causal, x, -jnp.inf) for x in s]
        m_old = [m_ref[h] for h in range(2)]
        m_new = [jnp.maximum(m_old[h], jnp.max(s[h], axis=1, keepdims=True)) for h in range(2)]
        p = [jnp.concatenate([jnp.exp2(s[h][:, c:c + LANES] - m_new[h]) for c in range(0, tq, LANES)],
                             axis=1).astype(BF16) for h in range(2)]
        pv = [_dot(p[h], va_ref[h, pl.ds(start, tq), :]) for h in range(2)]
        for h in range(2):
            acc_ref[h] = jnp.exp2(m_old[h] - m_new[h]) * acc_ref[h] + pv[h]
            m_ref[h] = m_new[h]

    def body(j, carry):
        step(j, False)
        return carry

    lax.fori_loop(0, i, body, 0)
    step(i, True)

    halves = []
    for h in range(2):
        acc = acc_ref[h]
        denom = jnp.sum(jnp.where(lane == HEAD_DIM * (1 - h), acc, 0.0), axis=1, keepdims=True)
        halves.append(acc / denom)
    o = jnp.where(lane < HEAD_DIM, halves[0], halves[1])
    ms = _head_sum(o * o, _head_ones()) * (1.0 / HEAD_DIM)
    gate = _sigmoid(og_ref[...].astype(F32))
    y_ref[...] = (o * lax.rsqrt(ms + NORM_EPS) * ow_ref[...] * gate).astype(y_ref.dtype)


def _fox_attn(qa, ka, va, og, ow, tq):
    b, _, t, _ = qa.shape
    kv_spec = pl.BlockSpec((None, 2, t, LANES), lambda bi, hp, qi: (bi, hp, 0, 0))
    return pl.pallas_call(
        functools.partial(_fox_attn_kernel, tq=tq),
        grid=(b, PAIRS, t // tq),
        in_specs=[pl.BlockSpec((None, 2, tq, LANES), lambda bi, hp, qi: (bi, hp, qi, 0)),
                  kv_spec, kv_spec,
                  pl.BlockSpec((None, tq, LANES), lambda bi, hp, qi: (bi, qi, hp)),
                  pl.BlockSpec((1, LANES), lambda bi, hp, qi: (0, 0))],
        out_specs=pl.BlockSpec((None, tq, LANES), lambda bi, hp, qi: (bi, qi, hp)),
        out_shape=jax.ShapeDtypeStruct((b, t, MIX_HALF), BF16),
        scratch_shapes=[pltpu.VMEM((2, tq, LANES), F32), pltpu.VMEM((2, tq, LANES), F32)],
        compiler_params=_params(("arbitrary", "arbitrary", "arbitrary")),
        name="fox_attn",
    )(qa, ka, va, og, ow)


def _ffn_kernel(x_ref, yr_ref, yf_ref, wo_ref, nf_ref, wu_ref, cw_ref, cb_ref, wd_ref, nfin_ref, o_ref,
                x2_ref, hid_ref, carry_ref, *, tm):
    @pl.when(pl.program_id(1) == 0)
    def _():
        carry_ref[...] = jnp.zeros_like(carry_ref)

    x2 = (x_ref[...] + _dot(yr_ref[...], wo_ref[:MIX_HALF, :]) + _dot(yf_ref[...], wo_ref[MIX_HALF:, :]))
    x2_ref[...] = x2
    ms = jnp.mean(x2 * x2, axis=-1, keepdims=True)
    h2 = (x2 * lax.rsqrt(ms + NORM_EPS) * nf_ref[...]).astype(BF16)

    def up(lo, w):
        return [_dot(h2, wu_ref[:, off + lo:off + lo + w]) for off in (0, D_FF)]

    def conv(u, cols):
        ext = jnp.concatenate([carry_ref[:, cols], u], axis=0)
        carry_ref[:, cols] = u[tm - 8:, :]
        u1 = pltpu.roll(ext, 1, axis=0)[8:, :]
        u2 = pltpu.roll(ext, 2, axis=0)[8:, :]
        cw = cw_ref[:, cols]
        return cw[0:1, :] * u2 + cw[1:2, :] * u1 + cw[2:3, :] * u + cb_ref[:, cols]

    def conv_glu(lo, w, u):
        gate = conv(u[0], slice(lo, lo + w))
        val = conv(u[1], slice(D_FF + lo, D_FF + lo + w))
        hid_ref[:, lo:lo + w] = (gate * _sigmoid(gate) * val).astype(BF16)

    chunks = [(lo, min(FF_CHUNK, D_FF - lo)) for lo in range(0, D_FF, FF_CHUNK)]
    u_prev = up(*chunks[0])
    for prev, cur in zip(chunks[:-1], chunks[1:]):
        u_next = up(*cur)
        conv_glu(*prev, u_prev)
        u_prev = u_next
    conv_glu(*chunks[-1], u_prev)

    xo = x2_ref[...] + _dot(hid_ref[...], wd_ref[...])
    ms = jnp.mean(xo * xo, axis=-1, keepdims=True)
    o_ref[...] = xo * lax.rsqrt(ms + NORM_EPS) * nfin_ref[...]


def _ffn(x3, yr, yf, wo, nf, wu, cw, cb, wd, nfin, tm):
    b, t, _ = x3.shape
    row = lambda w: pl.BlockSpec((None, tm, w), lambda bi, ti: (bi, ti, 0))
    return pl.pallas_call(
        functools.partial(_ffn_kernel, tm=tm),
        grid=(b, t // tm),
        in_specs=[row(D_MODEL), row(MIX_HALF), row(MIX_HALF)]
        + [_resident(a.shape) for a in (wo, nf, wu, cw, cb, wd, nfin)],
        out_specs=row(D_MODEL),
        out_shape=jax.ShapeDtypeStruct(x3.shape, F32),
        scratch_shapes=[pltpu.VMEM((tm, D_MODEL), F32), pltpu.VMEM((tm, D_FF), BF16),
                        pltpu.VMEM((8, 2 * D_FF), F32)],
        compiler_params=_params(("arbitrary", "arbitrary")),
        name="outproj_convffn",
    )(x3, yr, yf, wo, nf, wu, cw, cb, wd, nfin)


def _pack_w_in(w_in):
    fox = w_in[:, RWKV_COLS:]
    fl = fox[:, 4 * MIX_HALF:]
    pad = jnp.zeros((D_MODEL, FOX_Q_COL - RWKV_COLS - fl.shape[1]), w_in.dtype)
    return jnp.concatenate([w_in[:, :RWKV_COLS], fl, pad, fox[:, :4 * MIX_HALF]], axis=1).astype(BF16)


def _tile(n, pref):
    t = min(n, pref)
    assert n % t == 0, (n, t)
    return t


def kernel(x, norm_mix_w, w_in, rwkv_mu, rwkv_w0, rwkv_w2, rwkv_a0, rwkv_a2, rwkv_g2, rwkv_k_k, rwkv_k_a,
           rwkv_r_k, rwkv_lnx_w, rwkv_lnx_b, fox_f_bias, fox_q_norm_w, fox_k_norm_w, fox_o_norm_w, w_out,
           norm_ffn_w, ffn_w_up, ffn_conv_w, ffn_conv_b, ffn_w_down, norm_final_w):
    b, t, d = x.shape
    assert d == D_MODEL and norm_mix_w.shape[0] == 1 and t % CHUNK == 0
    row = lambda a: a.reshape(1, -1).astype(F32)
    twice = lambda a: jnp.tile(a.reshape(1, HEAD_DIM), (1, 2)).astype(F32)

    fb = jnp.zeros((1, LANES), F32).at[0, :N_HEADS].set(fox_f_bias[0])
    pr, og, qa, ka, va = _inproj(x, row(norm_mix_w[0]), _pack_w_in(w_in[0]), fb, twice(fox_q_norm_w[0]),
                                 twice(fox_k_norm_w[0]), _tile(t, TILE_INPROJ))

    zeros = jnp.zeros((DECAY_LORA, MIX_HALF), F32)
    w2p = jnp.concatenate([rwkv_w2[0], zeros], axis=0).astype(BF16)
    a2p = jnp.concatenate([zeros, rwkv_a2[0]], axis=0).astype(BF16)
    vec_params = [row(a[0]) for a in (rwkv_w0, rwkv_a0, rwkv_k_k, rwkv_k_a, rwkv_r_k, rwkv_lnx_w, rwkv_lnx_b)]
    y_rwkv = _rwkv(pr, row(rwkv_mu[0]), vec_params, w2p, a2p, rwkv_g2[0].astype(BF16), _tile(t, TILE_RWKV))

    y_fox = _fox_attn(qa, ka, va, og, twice(fox_o_norm_w[0]), _tile(t, TILE_FOX_ATTN))

    out = _ffn(x, y_rwkv, y_fox, w_out[0].astype(BF16), row(norm_ffn_w[0]),
               ffn_w_up[0].astype(BF16), ffn_conv_w[0].astype(F32), row(ffn_conv_b[0]),
               ffn_w_down[0].astype(BF16), row(norm_final_w),
               _tile(t, TILE_FFN))
    return out.astype(x.dtype)
```

```python
import functools

import jax
import jax.numpy as jnp
import numpy as np
from jax import lax
from jax.experimental import pallas as pl
from jax.experimental.pallas import tpu as pltpu

F32 = jnp.float32
BF16 = jnp.bfloat16

D_MODEL = 1024
HEAD_DIM = 64
N_HEADS = 8
MIX_HALF = N_HEADS * HEAD_DIM
DECAY_LORA = 64
AAA_LORA = 64
GATE_LORA = 128
RWKV_COLS = 3 * MIX_HALF + DECAY_LORA + AAA_LORA + GATE_LORA
D_FF = 2816
NORM_EPS = 1e-6
LNX_EPS = 64e-5

LANES = 128
PAIRS = MIX_HALF // LANES
CHUNK = 64
FL_BLOCK = RWKV_COLS // LANES
FOX_Q_COL = 2048
FF_CHUNK = 512
CUMSUM_ROWS = 256
VMEM_LIMIT = 56 * 1024 * 1024
LOG2E = 1.4426950408889634

TILE_INPROJ = 512
TILE_RWKV = 256
TILE_FOX_ATTN = 512
TILE_FFN = 512


def _params(sem):
    return pltpu.CompilerParams(dimension_semantics=sem, vmem_limit_bytes=VMEM_LIMIT)


def _split3(x):
    hi = x.astype(BF16)
    r1 = x - hi.astype(F32)
    mid = r1.astype(BF16)
    lo = (r1 - mid.astype(F32)).astype(BF16)
    return hi, mid, lo


def _dot(a, b, dims=None):
    if dims is None:
        dims = (((a.ndim - 1,), (0,)), ((), ()))
    return lax.dot_general(a, b, dims, preferred_element_type=F32)


_NT = (((1,), (1,)), ((), ()))

_BNN = (((2,), (1,)), ((0,), (0,)))
_BNT = (((2,), (2,)), ((0,), (0,)))
_BTN = (((1,), (1,)), ((0,), (0,)))


def _bmm(a, b, dims=_BNN):
    return lax.dot_general(a.astype(BF16), b.astype(BF16), dims, preferred_element_type=F32)


def _dot_exact_lhs(lhs_bf16, x):
    hi, mid, lo = _split3(x)
    return _dot(lhs_bf16, hi) + _dot(lhs_bf16, mid) + _dot(lhs_bf16, lo)


def _iota(shape, dim):
    return lax.broadcasted_iota(jnp.int32, shape, dim)


def _head_ones():
    r = _iota((LANES, LANES), 0) // HEAD_DIM
    c = _iota((LANES, LANES), 1) // HEAD_DIM
    return jnp.where(r == c, 1.0, 0.0).astype(BF16)


def _head_sum(x, ones_h):
    xb = x.astype(BF16)
    return jnp.concatenate([_dot(xb[:, i:i + LANES], ones_h) for i in range(0, x.shape[1], LANES)], axis=1)


def _block_tri(n):
    r = _iota((n, n), 0)
    c = _iota((n, n), 1)
    return jnp.where((r // CHUNK == c // CHUNK) & (c <= r), 1.0, 0.0).astype(BF16)


def _sigmoid(z):
    return 1.0 / (1.0 + jnp.exp(-z))


def _softplus(z):
    return jnp.maximum(z, 0.0) + jnp.log1p(jnp.exp(-jnp.abs(z)))


def _resident(shape):
    return pl.BlockSpec(shape, lambda *_: (0,) * len(shape), pipeline_mode=pl.Buffered(1))


def _aug_tables():
    sel = np.zeros((LANES, N_HEADS * LANES), np.float32)
    for h in range(N_HEADS):
        a0 = h * LANES + HEAD_DIM * (1 - h % 2)
        for i in range(3):
            sel[i * N_HEADS + h, a0 + i] = 1.0
            sel[i * N_HEADS + h, a0 + 3 + i] = -1.0
    return jnp.asarray(sel, BF16)


def _inproj_kernel(x_ref, nw_ref, w_ref, fb_ref, qw_ref, kw_ref, sel_ref,
                   mu_ref, w0_ref, a0_ref, kkw_ref, kaw_ref, rkw_ref, w2_ref, a2_ref, g2_ref,
                   ops_ref, bg_ref, wc_ref, og_ref, qa_ref, ka_ref, va_ref, carry_ref, prev_ref, *, tm):
    @pl.when(pl.program_id(1) == 0)
    def _():
        carry_ref[...] = jnp.zeros_like(carry_ref)
        prev_ref[...] = jnp.zeros_like(prev_ref)

    x = x_ref[...]
    ms = jnp.mean(x * x, axis=-1, keepdims=True)
    h = (x * lax.rsqrt(ms + NORM_EPS) * nw_ref[...]).astype(BF16)
    half_w = MIX_HALF // 2
    proj = {}

    def project(name, lo, width=half_w):
        proj[name] = _dot(h, w_ref[:, lo:lo + width])

    def matmuls():
        project("fl", FL_BLOCK * LANES, LANES)
        for j, name in enumerate(("fq0", "fk0", "fv0")):
            project(name, FOX_Q_COL + j * MIX_HALF)
        yield
        for j, name in enumerate(("fq1", "fk1", "fv1")):
            project(name, FOX_Q_COL + j * MIX_HALF + half_w)
            yield
        project("wg", 3 * MIX_HALF)
        yield
        for j, name in ((1, "rk0"), (1, "rk1"), (0, "rr0"), (0, "rr1"), (2, "rv0"), (2, "rv1")):
            project(name, j * MIX_HALF + int(name[-1]) * half_w)
            yield
        og = FOX_Q_COL + 3 * MIX_HALF
        og_ref[:, :half_w] = _dot(h, w_ref[:, og:og + half_w]).astype(BF16)
        yield
        og_ref[:, half_w:] = _dot(h, w_ref[:, og + half_w:og + MIX_HALF]).astype(BF16)

    lane = _iota((1, LANES), 1)
    ones_h = _head_ones()

    def shifted(name, lo):
        p = proj.pop(name)
        cols = slice(lo, lo + p.shape[1])
        prev = jnp.where(_iota(p.shape, 0) == 0, prev_ref[7:8, cols], pltpu.roll(p, 1, axis=0))
        prev_ref[:, cols] = p[tm - 8:, :]
        return p + (prev - p) * mu_ref[:, cols]

    def prepare():
        logf = -_softplus(-(proj.pop("fl") + fb_ref[...]))
        blk = min(tm, CUMSUM_ROWS)
        tri = jnp.where(_iota((blk, blk), 1) <= _iota((blk, blk), 0), 1.0, 0.0).astype(BF16)
        run = carry_ref[0:1, :]
        c_blocks = []
        for r0 in range(0, tm, blk):
            cb = _dot_exact_lhs(tri, logf[r0:r0 + blk, :]) + run
            run = cb[blk - 1:blk, :]
            c_blocks.append(cb)
        carry_ref[...] = jnp.broadcast_to(run, carry_ref.shape)
        c = jnp.concatenate(c_blocks, axis=0)
        c_hi, c_mid, c_lo = (p.astype(F32) for p in _split3(c * LOG2E))
        packed = jnp.where(lane < N_HEADS, c_hi, jnp.where(lane < 2 * N_HEADS, pltpu.roll(c_mid, N_HEADS, axis=1),
                                                           pltpu.roll(c_lo, 2 * N_HEADS, axis=1)))
        aug = _dot(packed.astype(BF16), sel_ref[...])
        r2 = _iota((2 * LANES, 2 * LANES), 0) // HEAD_DIM
        c2 = _iota((2 * LANES, 2 * LANES), 1) // HEAD_DIM
        ones_qk = jnp.where(r2 == c2, 1.0, 0.0).astype(BF16)
        yield
        for pb in range(PAIRS):
            sl = slice((pb % 2) * LANES, (pb % 2 + 1) * LANES)
            q, k, v = (proj[name + str(pb // 2)][:, sl] for name in ("fq", "fk", "fv"))
            ss = _dot(jnp.concatenate([q * q, k * k], axis=1).astype(BF16), ones_qk) * (1.0 / HEAD_DIM)
            qn = q * lax.rsqrt(ss[:, :LANES] + NORM_EPS) * (qw_ref[...] * (HEAD_DIM ** -0.5 * LOG2E))
            kn = k * lax.rsqrt(ss[:, LANES:] + NORM_EPS) * kw_ref[...]
            for half in range(2):
                hd = 2 * pb + half
                own = (lane // HEAD_DIM) == half
                a0 = HEAD_DIM * (1 - half)
                z = aug[:, hd * LANES:(hd + 1) * LANES]
                c_q = (lane >= a0) & (lane < a0 + 3)
                c_k = (lane >= a0 + 3) & (lane < a0 + 6)
                qa = jnp.where(own, qn, jnp.where(c_q, z, jnp.where(c_k, 1.0, 0.0)))
                ka = jnp.where(own, kn, jnp.where(c_k, z, jnp.where(c_q, 1.0, 0.0)))
                va = jnp.where(own, v, jnp.where(lane == a0, 1.0, 0.0))
                qa_ref[hd] = qa.astype(BF16)
                ka_ref[hd] = ka.astype(BF16)
                va_ref[hd] = va.astype(BF16)
            yield

        wg = shifted("wg", 3 * MIX_HALF)
        tanh_wa = jnp.tanh(wg[:, :LANES]).astype(BF16)
        wa = wg[:, :LANES].astype(BF16)
        sig_gl = _sigmoid(wg[:, LANES:]).astype(BF16)
        tri = _block_tri(blk)
        st = []
        for j in range(2):
            cj = slice(j * half_w, (j + 1) * half_w)
            w = -_softplus(-(w0_ref[:, cj] + _dot(tanh_wa, w2_ref[:, cj]))) - 0.5
            logd = -jnp.exp(w)
            a_sig = _sigmoid(a0_ref[:, cj] + _dot(wa, a2_ref[:, cj]))
            bg_ref[:, MIX_HALF + j * half_w:MIX_HALF + (j + 1) * half_w] = _dot(sig_gl, g2_ref[:, cj]).astype(BF16)
            cs = jnp.concatenate([_dot_exact_lhs(tri, logd[r0:r0 + blk, :]) for r0 in range(0, tm, blk)], axis=0)
            w_incl = jnp.exp(cs)
            wc_ref[:, :, cj] = w_incl.reshape(tm // CHUNK, CHUNK, half_w)[:, CHUNK - 1:, :]
            st.append(dict(a_sig=a_sig, w_incl=w_incl, w_inv=jnp.exp(-cs), w_excl=jnp.exp(cs - logd)))
            yield
        group = lambda g, j: slice(g * MIX_HALF + j * half_w, g * MIX_HALF + (j + 1) * half_w)
        for j in range(2):
            k = shifted("rk%d" % j, MIX_HALF + j * half_w)
            kk = k * kkw_ref[:, group(0, j)]
            kk = kk * lax.rsqrt(jnp.maximum(_head_sum(kk * kk, ones_h), 1e-24))
            st[j]["k2"] = k * (1.0 + (st[j]["a_sig"] - 1.0) * kaw_ref[:, group(0, j)])
            ops_ref[:, group(0, j)] = ((-kk) * st[j]["w_excl"]).astype(BF16)
            ops_ref[:, group(2, j)] = (kk * st[j]["a_sig"] * st[j]["w_inv"]).astype(BF16)
            ops_ref[:, group(3, j)] = (st[j]["k2"] * st[j]["w_inv"]).astype(BF16)
            yield
        for j in range(2):
            st[j]["r"] = shifted("rr%d" % j, j * half_w)
            ops_ref[:, group(1, j)] = (st[j]["r"] * st[j]["w_incl"]).astype(BF16)
            yield
        for j in range(2):
            v = shifted("rv%d" % j, 2 * MIX_HALF + j * half_w)
            ops_ref[:, group(4, j)] = v.astype(BF16)
            bonus = _head_sum(st[j]["r"] * st[j]["k2"] * rkw_ref[:, group(0, j)], ones_h) * v
            bg_ref[:, group(0, j)] = bonus.astype(BF16)
            yield

    streams = [matmuls(), prepare()]
    while streams:
        for g in list(streams):
            if next(g, streams) is streams:
                streams.remove(g)


def _inproj(x3, norm_w, w_packed, f_bias, qw, kw, rwkv_consts, tm):
    b, t, _ = x3.shape
    row = lambda w: pl.BlockSpec((None, tm, w), lambda bi, ti: (bi, ti, 0))
    head_spec = pl.BlockSpec((None, N_HEADS, tm, LANES), lambda bi, ti: (bi, 0, ti, 0))
    head_shape = jax.ShapeDtypeStruct((b, N_HEADS, t, LANES), BF16)
    consts = (norm_w, w_packed, f_bias, qw, kw, _aug_tables()) + tuple(rwkv_consts)
    return pl.pallas_call(
        functools.partial(_inproj_kernel, tm=tm),
        grid=(b, t // tm),
        in_specs=[row(D_MODEL)] + [_resident(a.shape) for a in consts],
        out_specs=[row(5 * MIX_HALF), row(2 * MIX_HALF),
                   pl.BlockSpec((None, tm // CHUNK, 1, MIX_HALF), lambda bi, ti: (bi, ti, 0, 0)),
                   row(MIX_HALF), head_spec, head_spec, head_spec],
        out_shape=[jax.ShapeDtypeStruct((b, t, 5 * MIX_HALF), BF16), jax.ShapeDtypeStruct((b, t, 2 * MIX_HALF), BF16),
                   jax.ShapeDtypeStruct((b, t // CHUNK, 1, MIX_HALF), F32),
                   jax.ShapeDtypeStruct((b, t, MIX_HALF), BF16), head_shape, head_shape, head_shape],
        scratch_shapes=[pltpu.VMEM((8, LANES), F32), pltpu.VMEM((8, RWKV_COLS), F32)],
        compiler_params=_params(("arbitrary", "arbitrary")),
        name="inproj_foxprep",
    )(x3, *consts)


def _rwkv_kernel(a_ref, r_ref, b_ref, k_ref, v_ref, bonus_ref, gate_ref, wc_ref, lnw_ref, lnb_ref,
                 y_ref, s_ref, *, tt):
    @pl.when(pl.program_id(1) == 0)
    def _():
        s_ref[...] = jnp.zeros_like(s_ref)

    nc = tt // CHUNK
    ones_h = _head_ones()
    m0 = _iota((1, 1, LANES), 2) < HEAD_DIM

    def units(x_ref):
        out = []
        for p in range(PAIRS):
            xp = x_ref[:, p * LANES:(p + 1) * LANES].reshape(nc, CHUNK, LANES)
            zero = jnp.zeros_like(xp)
            out.append(jnp.concatenate([jnp.where(m0, xp, zero), jnp.where(m0, zero, xp)], axis=1))
        return jnp.concatenate(out, axis=0)

    a_st, r_st, b_st, k_st, v_st = (units(ref) for ref in (a_ref, r_ref, b_ref, k_ref, v_ref))
    wc = jnp.concatenate([wc_ref[:, :, p * LANES:(p + 1) * LANES] for p in range(PAIRS)], axis=0)

    ri = _iota((1, LANES, LANES), 1)
    ci = _iota((1, LANES, LANES), 2)
    same = (ri // CHUNK) == (ci // CHUNK)
    strict = same & (ci < ri)
    incl = same & (ci <= ri)
    eye = jnp.where(ri == ci, 1.0, 0.0).astype(F32)

    bk = jnp.concatenate([b_st, k_st], axis=1)
    sc = _bmm(jnp.concatenate([a_st, r_st], axis=1), bk, _BNT)
    l_ab = jnp.where(strict, sc[:, :LANES, :LANES], 0.0)
    l_ak = jnp.where(strict, sc[:, :LANES, LANES:], 0.0)
    m_rbk = jnp.concatenate([jnp.where(incl, sc[:, LANES:, :LANES], 0.0),
                             jnp.where(incl, sc[:, LANES:, LANES:], 0.0)], axis=2).astype(BF16)

    t_inv = eye + l_ab
    lp = l_ab.astype(BF16)
    lp = _bmm(lp, lp).astype(BF16)
    for _ in range(CHUNK.bit_length() - 3):
        both = _bmm(jnp.concatenate([lp, t_inv.astype(BF16)], axis=1), lp)
        t_inv = t_inv + both[:, LANES:]
        lp = both[:, :LANES].astype(BF16)
    t_inv = t_inv + _bmm(t_inv, lp)

    lakv = _bmm(l_ak, v_st)
    au = _bmm(t_inv, jnp.concatenate([a_st, lakv.astype(BF16)], axis=2))
    au_b = au.astype(BF16)
    zv = jnp.concatenate([jnp.zeros_like(v_st), v_st], axis=2)
    ry = _bmm(m_rbk, jnp.concatenate([au_b, zv], axis=1))
    r2_st = r_st.astype(F32) + ry[:, :, :LANES]
    y0_st = ry[:, :, LANES:]
    g_mat = (eye + _bmm(au_b[:, :, :LANES], b_st, _BTN)) * wc
    h_mat = _bmm(jnp.concatenate([au_b[:, :, LANES:], v_st], axis=1), bk, _BTN) * wc

    def chunk(x, c):
        return x.reshape((PAIRS, nc) + x.shape[1:])[:, c]

    s = s_ref[...]
    ys = []
    for c in range(nc):
        y_st = _bmm(chunk(r2_st, c), s, _BNT) + chunk(y0_st, c)
        s = _bmm(s, chunk(g_mat, c)) + chunk(h_mat, c)
        ys.append(y_st[:, :CHUNK] + y_st[:, CHUNK:])
    s_ref[...] = s
    y = jnp.concatenate([jnp.concatenate([yc[p] for yc in ys], axis=0) for p in range(PAIRS)], axis=1)

    mean = _head_sum(y, ones_h) * (1.0 / HEAD_DIM)
    yc = y - mean
    var = _head_sum(yc * yc, ones_h) * (1.0 / HEAD_DIM)
    yn = yc * lax.rsqrt(var + LNX_EPS) * lnw_ref[...] + lnb_ref[...]
    y_ref[...] = ((yn + bonus_ref[...].astype(F32)) * gate_ref[...].astype(F32)).astype(y_ref.dtype)


def _rwkv(ops, bg, wc, lnw, lnb, tt):
    b, t, _ = ops.shape
    col = lambda cb: pl.BlockSpec((None, tt, MIX_HALF), lambda bi, ti, cb=cb: (bi, ti, cb))
    vec = pl.BlockSpec((1, MIX_HALF), lambda bi, ti: (0, 0))
    return pl.pallas_call(
        functools.partial(_rwkv_kernel, tt=tt),
        grid=(b, t // tt),
        in_specs=[col(i) for i in range(5)] + [col(0), col(1)]
        + [pl.BlockSpec((None, tt // CHUNK, 1, MIX_HALF), lambda bi, ti: (bi, ti, 0, 0)), vec, vec],
        out_specs=pl.BlockSpec((None, tt, MIX_HALF), lambda bi, ti: (bi, ti, 0)),
        out_shape=jax.ShapeDtypeStruct((b, t, MIX_HALF), BF16),
        scratch_shapes=[pltpu.VMEM((PAIRS, LANES, LANES), F32)],
        compiler_params=_params(("arbitrary", "arbitrary")),
        name="rwkv7",
    )(*([ops] * 5), bg, bg, wc, lnw, lnb)


def _fox_attn_kernel(qa_ref, ka_ref, va_ref, og_ref, ow_ref, y_ref, m_ref, acc_ref, *, tq):
    i = pl.program_id(2)
    lane = _iota((1, LANES), 1)
    causal = _iota((tq, tq), 1) <= _iota((tq, tq), 0)
    m_ref[...] = jnp.full(m_ref.shape, -jnp.inf, F32)
    acc_ref[...] = jnp.zeros_like(acc_ref)
    q = [qa_ref[0], qa_ref[1]]

    def step(j, masked):
        start = pl.multiple_of(j * tq, tq)
        s = [_dot(q[h], ka_ref[h, pl.ds(start, tq), :], _NT) for h in range(2)]
        if masked:
            s = [jnp.where(causal, x, -jnp.inf) for x in s]
        m_old = [m_ref[h] for h in range(2)]
        m_new = [jnp.maximum(m_old[h], jnp.max(s[h], axis=1, keepdims=True)) for h in range(2)]
        p = [jnp.concatenate([jnp.exp2(s[h][:, c:c + LANES] - m_new[h]) for c in range(0, tq, LANES)],
                             axis=1).astype(BF16) for h in range(2)]
        pv = [_dot(p[h], va_ref[h, pl.ds(start, tq), :]) for h in range(2)]
        for h in range(2):
            acc_ref[h] = jnp.exp2(m_old[h] - m_new[h]) * acc_ref[h] + pv[h]
            m_ref[h] = m_new[h]

    def body(j, carry):
        step(j, False)
        return carry

    lax.fori_loop(0, i, body, 0)
    step(i, True)

    halves = []
    for h in range(2):
        acc = acc_ref[h]
        denom = jnp.sum(jnp.where(lane == HEAD_DIM * (1 - h), acc, 0.0), axis=1, keepdims=True)
        halves.append(acc / denom)
    o = jnp.where(lane < HEAD_DIM, halves[0], halves[1])
    ms = _head_sum(o * o, _head_ones()) * (1.0 / HEAD_DIM)
    gate = _sigmoid(og_ref[...].astype(F32))
    y_ref[...] = (o * lax.rsqrt(ms + NORM_EPS) * ow_ref[...] * gate).astype(y_ref.dtype)


def _fox_attn(qa, ka, va, og, ow, tq):
    b, _, t, _ = qa.shape
    kv_spec = pl.BlockSpec((None, 2, t, LANES), lambda bi, hp, qi: (bi, hp, 0, 0))
    return pl.pallas_call(
        functools.partial(_fox_attn_kernel, tq=tq),
        grid=(b, PAIRS, t // tq),
        in_specs=[pl.BlockSpec((None, 2, tq, LANES), lambda bi, hp, qi: (bi, hp, qi, 0)),
                  kv_spec, kv_spec,
                  pl.BlockSpec((None, tq, LANES), lambda bi, hp, qi: (bi, qi, hp)),
                  pl.BlockSpec((1, LANES), lambda bi, hp, qi: (0, 0))],
        out_specs=pl.BlockSpec((None, tq, LANES), lambda bi, hp, qi: (bi, qi, hp)),
        out_shape=jax.ShapeDtypeStruct((b, t, MIX_HALF), BF16),
        scratch_shapes=[pltpu.VMEM((2, tq, LANES), F32), pltpu.VMEM((2, tq, LANES), F32)],
        compiler_params=_params(("arbitrary", "arbitrary", "arbitrary")),
        name="fox_attn",
    )(qa, ka, va, og, ow)


def _ffn_kernel(x_ref, yr_ref, yf_ref, wo_ref, nf_ref, wu_ref, cw_ref, cb_ref, wd_ref, nfin_ref, o_ref,
                x2_ref, hid_ref, carry_ref, *, tm):
    @pl.when(pl.program_id(1) == 0)
    def _():
        carry_ref[...] = jnp.zeros_like(carry_ref)

    x2 = (x_ref[...] + _dot(yr_ref[...], wo_ref[:MIX_HALF, :]) + _dot(yf_ref[...], wo_ref[MIX_HALF:, :]))
    x2_ref[...] = x2
    ms = jnp.mean(x2 * x2, axis=-1, keepdims=True)
    h2 = (x2 * lax.rsqrt(ms + NORM_EPS) * nf_ref[...]).astype(BF16)

    def up(lo, w):
        return [_dot(h2, wu_ref[:, off + lo:off + lo + w]) for off in (0, D_FF)]

    def conv(u, cols):
        ext = jnp.concatenate([carry_ref[:, cols], u], axis=0)
        carry_ref[:, cols] = u[tm - 8:, :]
        u1 = pltpu.roll(ext, 1, axis=0)[8:, :]
        u2 = pltpu.roll(ext, 2, axis=0)[8:, :]
        cw = cw_ref[:, cols]
        return cw[0:1, :] * u2 + cw[1:2, :] * u1 + cw[2:3, :] * u + cb_ref[:, cols]

    def conv_glu(lo, w, u):
        gate = conv(u[0], slice(lo, lo + w))
        val = conv(u[1], slice(D_FF + lo, D_FF + lo + w))
        hid_ref[:, lo:lo + w] = (gate * _sigmoid(gate) * val).astype(BF16)

    chunks = [(lo, min(FF_CHUNK, D_FF - lo)) for lo in range(0, D_FF, FF_CHUNK)]
    u_prev = up(*chunks[0])
    for prev, cur in zip(chunks[:-1], chunks[1:]):
        u_next = up(*cur)
        conv_glu(*prev, u_prev)
        u_prev = u_next
    conv_glu(*chunks[-1], u_prev)

    xo = x2_ref[...] + _dot(hid_ref[...], wd_ref[...])
    ms = jnp.mean(xo * xo, axis=-1, keepdims=True)
    o_ref[...] = xo * lax.rsqrt(ms + NORM_EPS) * nfin_ref[...]


def _ffn(x3, yr, yf, wo, nf, wu, cw, cb, wd, nfin, tm):
    b, t, _ = x3.shape
    row = lambda w: pl.BlockSpec((None, tm, w), lambda bi, ti: (bi, ti, 0))
    return pl.pallas_call(
        functools.partial(_ffn_kernel, tm=tm),
        grid=(b, t // tm),
        in_specs=[row(D_MODEL), row(MIX_HALF), row(MIX_HALF)]
        + [_resident(a.shape) for a in (wo, nf, wu, cw, cb, wd, nfin)],
        out_specs=row(D_MODEL),
        out_shape=jax.ShapeDtypeStruct(x3.shape, F32),
        scratch_shapes=[pltpu.VMEM((tm, D_MODEL), F32), pltpu.VMEM((tm, D_FF), BF16),
                        pltpu.VMEM((8, 2 * D_FF), F32)],
        compiler_params=_params(("arbitrary", "arbitrary")),
        name="outproj_convffn",
    )(x3, yr, yf, wo, nf, wu, cw, cb, wd, nfin)


def _pack_w_in(w_in):
    fox = w_in[:, RWKV_COLS:]
    fl = fox[:, 4 * MIX_HALF:]
    pad = jnp.zeros((D_MODEL, FOX_Q_COL - RWKV_COLS - fl.shape[1]), w_in.dtype)
    return jnp.concatenate([w_in[:, :RWKV_COLS], fl, pad, fox[:, :4 * MIX_HALF]], axis=1).astype(BF16)


def _tile(n, pref):
    t = min(n, pref)
    assert n % t == 0, (n, t)
    return t


def kernel(x, norm_mix_w, w_in, rwkv_mu, rwkv_w0, rwkv_w2, rwkv_a0, rwkv_a2, rwkv_g2, rwkv_k_k, rwkv_k_a,
           rwkv_r_k, rwkv_lnx_w, rwkv_lnx_b, fox_f_bias, fox_q_norm_w, fox_k_norm_w, fox_o_norm_w, w_out,
           norm_ffn_w, ffn_w_up, ffn_conv_w, ffn_conv_b, ffn_w_down, norm_final_w):
    b, t, d = x.shape
    assert d == D_MODEL and norm_mix_w.shape[0] == 1 and t % CHUNK == 0
    row = lambda a: a.reshape(1, -1).astype(F32)
    twice = lambda a: jnp.tile(a.reshape(1, HEAD_DIM), (1, 2)).astype(F32)

    fb = jnp.zeros((1, LANES), F32).at[0, :N_HEADS].set(fox_f_bias[0])
    zeros = jnp.zeros((DECAY_LORA, MIX_HALF), F32)
    w2p = jnp.concatenate([rwkv_w2[0], zeros], axis=0).astype(BF16)
    a2p = jnp.concatenate([zeros, rwkv_a2[0]], axis=0).astype(BF16)
    rwkv_consts = [row(a[0]) for a in (rwkv_mu, rwkv_w0, rwkv_a0, rwkv_k_k, rwkv_k_a, rwkv_r_k)]
    rwkv_consts += [w2p, a2p, rwkv_g2[0].astype(BF16)]
    ops, bg, wc, og, qa, ka, va = _inproj(x, row(norm_mix_w[0]), _pack_w_in(w_in[0]), fb, twice(fox_q_norm_w[0]),
                                          twice(fox_k_norm_w[0]), rwkv_consts, _tile(t, TILE_INPROJ))

    y_rwkv = _rwkv(ops, bg, wc, row(rwkv_lnx_w[0]), row(rwkv_lnx_b[0]), _tile(t, TILE_RWKV))

    y_fox = _fox_attn(qa, ka, va, og, twice(fox_o_norm_w[0]), _tile(t, TILE_FOX_ATTN))

    out = _ffn(x, y_rwkv, y_fox, w_out[0].astype(BF16), row(norm_ffn_w[0]),
               ffn_w_up[0].astype(BF16), ffn_conv_w[0].astype(F32), row(ffn_conv_b[0]),
               ffn_w_down[0].astype(BF16), row(norm_final_w),
               _tile(t, TILE_FFN))
    return out.astype(x.dtype)
```

```python
import functools

import jax
import jax.numpy as jnp
import numpy as np
from jax import lax
from jax.experimental import pallas as pl
from jax.experimental.pallas import tpu as pltpu

F32 = jnp.float32
BF16 = jnp.bfloat16

D_MODEL = 1024
HEAD_DIM = 64
N_HEADS = 8
MIX_HALF = N_HEADS * HEAD_DIM
DECAY_LORA = 64
AAA_LORA = 64
GATE_LORA = 128
RWKV_COLS = 3 * MIX_HALF + DECAY_LORA + AAA_LORA + GATE_LORA
D_FF = 2816
NORM_EPS = 1e-6
LNX_EPS = 64e-5

LANES = 128
PAIRS = MIX_HALF // LANES
CHUNK = 64
FL_BLOCK = RWKV_COLS // LANES
FOX_Q_COL = 2048
FF_CHUNK = 512
CUMSUM_ROWS = 256
VMEM_LIMIT = 56 * 1024 * 1024
LOG2E = 1.4426950408889634

TILE_INPROJ = 512
TILE_RWKV = 512
TILE_FOX_ATTN = 512
TILE_FFN = 512


def _params(sem):
    return pltpu.CompilerParams(dimension_semantics=sem, vmem_limit_bytes=VMEM_LIMIT)


def _split3(x):
    hi = x.astype(BF16)
    r1 = x - hi.astype(F32)
    mid = r1.astype(BF16)
    lo = (r1 - mid.astype(F32)).astype(BF16)
    return hi, mid, lo


def _dot(a, b, dims=None):
    if dims is None:
        dims = (((a.ndim - 1,), (0,)), ((), ()))
    return lax.dot_general(a, b, dims, preferred_element_type=F32)


_NT = (((1,), (1,)), ((), ()))

_BNN = (((2,), (1,)), ((0,), (0,)))
_BNT = (((2,), (2,)), ((0,), (0,)))
_BTN = (((1,), (1,)), ((0,), (0,)))


def _bmm(a, b, dims=_BNN):
    return lax.dot_general(a.astype(BF16), b.astype(BF16), dims, preferred_element_type=F32)


def _dot_exact_lhs(lhs_bf16, x):
    hi, mid, lo = _split3(x)
    return _dot(lhs_bf16, hi) + _dot(lhs_bf16, mid) + _dot(lhs_bf16, lo)


def _iota(shape, dim):
    return lax.broadcasted_iota(jnp.int32, shape, dim)


def _head_ones():
    r = _iota((LANES, LANES), 0) // HEAD_DIM
    c = _iota((LANES, LANES), 1) // HEAD_DIM
    return jnp.where(r == c, 1.0, 0.0).astype(BF16)


def _head_sum(x, ones_h):
    xb = x.astype(BF16)
    return jnp.concatenate([_dot(xb[:, i:i + LANES], ones_h) for i in range(0, x.shape[1], LANES)], axis=1)


def _block_tri(n):
    r = _iota((n, n), 0)
    c = _iota((n, n), 1)
    return jnp.where((r // CHUNK == c // CHUNK) & (c <= r), 1.0, 0.0).astype(BF16)


def _sigmoid(z):
    return 1.0 / (1.0 + jnp.exp(-z))


def _softplus(z):
    return jnp.maximum(z, 0.0) + jnp.log1p(jnp.exp(-jnp.abs(z)))


def _resident(shape):
    return pl.BlockSpec(shape, lambda *_: (0,) * len(shape), pipeline_mode=pl.Buffered(1))


def _aug_tables():
    sel = np.zeros((LANES, N_HEADS * LANES), np.float32)
    for h in range(N_HEADS):
        a0 = h * LANES + HEAD_DIM * (1 - h % 2)
        for i in range(3):
            sel[i * N_HEADS + h, a0 + i] = 1.0
            sel[i * N_HEADS + h, a0 + 3 + i] = -1.0
    return jnp.asarray(sel, BF16)


def _inproj_kernel(x_ref, nw_ref, w_ref, fb_ref, qw_ref, kw_ref, sel_ref,
                   mu_ref, w0_ref, a0_ref, kkw_ref, kaw_ref, rkw_ref, w2_ref, a2_ref, g2_ref,
                   ops_ref, bg_ref, wc_ref, og_ref, qa_ref, ka_ref, va_ref, carry_ref, prev_ref, *, tm):
    @pl.when(pl.program_id(1) == 0)
    def _():
        carry_ref[...] = jnp.zeros_like(carry_ref)
        prev_ref[...] = jnp.zeros_like(prev_ref)

    x = x_ref[...]
    ms = jnp.mean(x * x, axis=-1, keepdims=True)
    h = (x * lax.rsqrt(ms + NORM_EPS) * nw_ref[...]).astype(BF16)
    half_w = MIX_HALF // 2
    proj = {}

    def project(name, lo, width=half_w):
        proj[name] = _dot(h, w_ref[:, lo:lo + width])

    def matmuls():
        project("fl", FL_BLOCK * LANES, LANES)
        for j, name in enumerate(("fq0", "fk0", "fv0")):
            project(name, FOX_Q_COL + j * MIX_HALF)
        yield
        for j, name in enumerate(("fq1", "fk1", "fv1")):
            project(name, FOX_Q_COL + j * MIX_HALF + half_w)
            yield
        project("wg", 3 * MIX_HALF)
        yield
        for j, name in ((1, "rk0"), (1, "rk1"), (0, "rr0"), (0, "rr1"), (2, "rv0"), (2, "rv1")):
            project(name, j * MIX_HALF + int(name[-1]) * half_w)
            yield
        og = FOX_Q_COL + 3 * MIX_HALF
        og_ref[:, :half_w] = _dot(h, w_ref[:, og:og + half_w]).astype(BF16)
        yield
        og_ref[:, half_w:] = _dot(h, w_ref[:, og + half_w:og + MIX_HALF]).astype(BF16)

    lane = _iota((1, LANES), 1)
    ones_h = _head_ones()

    def shifted(name, lo):
        p = proj.pop(name)
        cols = slice(lo, lo + p.shape[1])
        prev = jnp.where(_iota(p.shape, 0) == 0, prev_ref[7:8, cols], pltpu.roll(p, 1, axis=0))
        prev_ref[:, cols] = p[tm - 8:, :]
        return p + (prev - p) * mu_ref[:, cols]

    def prepare():
        logf = -_softplus(-(proj.pop("fl") + fb_ref[...]))
        blk = min(tm, CUMSUM_ROWS)
        tri = jnp.where(_iota((blk, blk), 1) <= _iota((blk, blk), 0), 1.0, 0.0).astype(BF16)
        run = carry_ref[0:1, :]
        c_blocks = []
        for r0 in range(0, tm, blk):
            cb = _dot_exact_lhs(tri, logf[r0:r0 + blk, :]) + run
            run = cb[blk - 1:blk, :]
            c_blocks.append(cb)
        carry_ref[...] = jnp.broadcast_to(run, carry_ref.shape)
        c = jnp.concatenate(c_blocks, axis=0)
        c_hi, c_mid, c_lo = (p.astype(F32) for p in _split3(c * LOG2E))
        packed = jnp.where(lane < N_HEADS, c_hi, jnp.where(lane < 2 * N_HEADS, pltpu.roll(c_mid, N_HEADS, axis=1),
                                                           pltpu.roll(c_lo, 2 * N_HEADS, axis=1)))
        aug = _dot(packed.astype(BF16), sel_ref[...])
        r2 = _iota((2 * LANES, 2 * LANES), 0) // HEAD_DIM
        c2 = _iota((2 * LANES, 2 * LANES), 1) // HEAD_DIM
        ones_qk = jnp.where(r2 == c2, 1.0, 0.0).astype(BF16)
        yield
        for pb in range(PAIRS):
            sl = slice((pb % 2) * LANES, (pb % 2 + 1) * LANES)
            q, k, v = (proj[name + str(pb // 2)][:, sl] for name in ("fq", "fk", "fv"))
            ss = _dot(jnp.concatenate([q * q, k * k], axis=1).astype(BF16), ones_qk) * (1.0 / HEAD_DIM)
            qn = q * lax.rsqrt(ss[:, :LANES] + NORM_EPS) * (qw_ref[...] * (HEAD_DIM ** -0.5 * LOG2E))
            kn = k * lax.rsqrt(ss[:, LANES:] + NORM_EPS) * kw_ref[...]
            for half in range(2):
                hd = 2 * pb + half
                own = (lane // HEAD_DIM) == half
                a0 = HEAD_DIM * (1 - half)
                z = aug[:, hd * LANES:(hd + 1) * LANES]
                c_q = (lane >= a0) & (lane < a0 + 3)
                c_k = (lane >= a0 + 3) & (lane < a0 + 6)
                qa = jnp.where(own, qn, jnp.where(c_q, z, jnp.where(c_k, 1.0, 0.0)))
                ka = jnp.where(own, kn, jnp.where(c_k, z, jnp.where(c_q, 1.0, 0.0)))
                va = jnp.where(own, v, jnp.where(lane == a0, 1.0, 0.0))
                qa_ref[hd] = qa.astype(BF16)
                ka_ref[hd] = ka.astype(BF16)
                va_ref[hd] = va.astype(BF16)
            yield

        wg = shifted("wg", 3 * MIX_HALF)
        tanh_wa = jnp.tanh(wg[:, :LANES]).astype(BF16)
        wa = wg[:, :LANES].astype(BF16)
        sig_gl = _sigmoid(wg[:, LANES:]).astype(BF16)
        tri = _block_tri(blk)
        st = []
        for j in range(2):
            cj = slice(j * half_w, (j + 1) * half_w)
            w = -_softplus(-(w0_ref[:, cj] + _dot(tanh_wa, w2_ref[:, cj]))) - 0.5
            logd = -jnp.exp(w)
            a_sig = _sigmoid(a0_ref[:, cj] + _dot(wa, a2_ref[:, cj]))
            bg_ref[:, MIX_HALF + j * half_w:MIX_HALF + (j + 1) * half_w] = _dot(sig_gl, g2_ref[:, cj]).astype(BF16)
            cs = jnp.concatenate([_dot_exact_lhs(tri, logd[r0:r0 + blk, :]) for r0 in range(0, tm, blk)], axis=0)
            w_incl = jnp.exp(cs)
            wc_ref[:, :, cj] = w_incl.reshape(tm // CHUNK, CHUNK, half_w)[:, CHUNK - 1:, :]
            st.append(dict(a_sig=a_sig, w_incl=w_incl, w_inv=jnp.exp(-cs), w_excl=jnp.exp(cs - logd)))
            yield
        group = lambda g, j: slice(g * MIX_HALF + j * half_w, g * MIX_HALF + (j + 1) * half_w)
        for j in range(2):
            k = shifted("rk%d" % j, MIX_HALF + j * half_w)
            kk = k * kkw_ref[:, group(0, j)]
            kk = kk * lax.rsqrt(jnp.maximum(_head_sum(kk * kk, ones_h), 1e-24))
            st[j]["k2"] = k * (1.0 + (st[j]["a_sig"] - 1.0) * kaw_ref[:, group(0, j)])
            ops_ref[:, group(0, j)] = ((-kk) * st[j]["w_excl"]).astype(BF16)
            ops_ref[:, group(2, j)] = (kk * st[j]["a_sig"] * st[j]["w_inv"]).astype(BF16)
            ops_ref[:, group(3, j)] = (st[j]["k2"] * st[j]["w_inv"]).astype(BF16)
            yield
        for j in range(2):
            st[j]["r"] = shifted("rr%d" % j, j * half_w)
            ops_ref[:, group(1, j)] = (st[j]["r"] * st[j]["w_incl"]).astype(BF16)
            yield
        for j in range(2):
            v = shifted("rv%d" % j, 2 * MIX_HALF + j * half_w)
            ops_ref[:, group(4, j)] = v.astype(BF16)
            bonus = _head_sum(st[j]["r"] * st[j]["k2"] * rkw_ref[:, group(0, j)], ones_h) * v
            bg_ref[:, group(0, j)] = bonus.astype(BF16)
            yield

    streams = [matmuls(), prepare()]
    while streams:
        for g in list(streams):
            if next(g, streams) is streams:
                streams.remove(g)


def _inproj(x3, norm_w, w_packed, f_bias, qw, kw, rwkv_consts, tm):
    b, t, _ = x3.shape
    row = lambda w: pl.BlockSpec((None, tm, w), lambda bi, ti: (bi, ti, 0))
    head_spec = pl.BlockSpec((None, N_HEADS, tm, LANES), lambda bi, ti: (bi, 0, ti, 0))
    head_shape = jax.ShapeDtypeStruct((b, N_HEADS, t, LANES), BF16)
    consts = (norm_w, w_packed, f_bias, qw, kw, _aug_tables()) + tuple(rwkv_consts)
    return pl.pallas_call(
        functools.partial(_inproj_kernel, tm=tm),
        grid=(b, t // tm),
        in_specs=[row(D_MODEL)] + [_resident(a.shape) for a in consts],
        out_specs=[row(5 * MIX_HALF), row(2 * MIX_HALF),
                   pl.BlockSpec((None, tm // CHUNK, 1, MIX_HALF), lambda bi, ti: (bi, ti, 0, 0)),
                   row(MIX_HALF), head_spec, head_spec, head_spec],
        out_shape=[jax.ShapeDtypeStruct((b, t, 5 * MIX_HALF), BF16), jax.ShapeDtypeStruct((b, t, 2 * MIX_HALF), BF16),
                   jax.ShapeDtypeStruct((b, t // CHUNK, 1, MIX_HALF), F32),
                   jax.ShapeDtypeStruct((b, t, MIX_HALF), BF16), head_shape, head_shape, head_shape],
        scratch_shapes=[pltpu.VMEM((8, LANES), F32), pltpu.VMEM((8, RWKV_COLS), F32)],
        compiler_params=_params(("arbitrary", "arbitrary")),
        name="inproj_foxprep",
    )(x3, *consts)


def _rwkv_kernel(a_ref, r_ref, b_ref, k_ref, v_ref, bonus_ref, gate_ref, wc_ref, lnw_ref, lnb_ref,
                 y_ref, s_ref, *, tt):
    @pl.when(pl.program_id(1) == 0)
    def _():
        s_ref[...] = jnp.zeros_like(s_ref)

    nc = tt // CHUNK
    ones_h = _head_ones()
    m0 = _iota((1, 1, LANES), 2) < HEAD_DIM

    def units(x_ref):
        out = []
        for p in range(PAIRS):
            xp = x_ref[:, p * LANES:(p + 1) * LANES].reshape(nc, CHUNK, LANES)
            zero = jnp.zeros_like(xp)
            out.append(jnp.concatenate([jnp.where(m0, xp, zero), jnp.where(m0, zero, xp)], axis=1))
        return jnp.concatenate(out, axis=0)

    a_st, r_st, b_st, k_st, v_st = (units(ref) for ref in (a_ref, r_ref, b_ref, k_ref, v_ref))
    wc = jnp.concatenate([wc_ref[:, :, p * LANES:(p + 1) * LANES] for p in range(PAIRS)], axis=0)

    ri = _iota((1, LANES, LANES), 1)
    ci = _iota((1, LANES, LANES), 2)
    same = (ri // CHUNK) == (ci // CHUNK)
    strict = same & (ci < ri)
    incl = same & (ci <= ri)
    eye = jnp.where(ri == ci, 1.0, 0.0).astype(F32)

    bk = jnp.concatenate([b_st, k_st], axis=1)
    sc = _bmm(jnp.concatenate([a_st, r_st], axis=1), bk, _BNT)
    l_ab = jnp.where(strict, sc[:, :LANES, :LANES], 0.0)
    l_ak = jnp.where(strict, sc[:, :LANES, LANES:], 0.0)
    m_rbk = jnp.concatenate([jnp.where(incl, sc[:, LANES:, :LANES], 0.0),
                             jnp.where(incl, sc[:, LANES:, LANES:], 0.0)], axis=2).astype(BF16)

    t_inv = eye + l_ab
    lp = l_ab.astype(BF16)
    lp = _bmm(lp, lp).astype(BF16)
    for _ in range(CHUNK.bit_length() - 3):
        both = _bmm(jnp.concatenate([lp, t_inv.astype(BF16)], axis=1), lp)
        t_inv = t_inv + both[:, LANES:]
        lp = both[:, :LANES].astype(BF16)
    t_inv = t_inv + _bmm(t_inv, lp)

    lakv = _bmm(l_ak, v_st)
    au = _bmm(t_inv, jnp.concatenate([a_st, lakv.astype(BF16)], axis=2))
    au_b = au.astype(BF16)
    zv = jnp.concatenate([jnp.zeros_like(v_st), v_st], axis=2)
    ry = _bmm(m_rbk, jnp.concatenate([au_b, zv], axis=1))
    r2_st = r_st.astype(F32) + ry[:, :, :LANES]
    y0_st = ry[:, :, LANES:]
    g_mat = (eye + _bmm(au_b[:, :, :LANES], b_st, _BTN)) * wc
    h_mat = _bmm(jnp.concatenate([au_b[:, :, LANES:], v_st], axis=1), bk, _BTN) * wc

    def chunk(x, c):
        return x.reshape((PAIRS, nc) + x.shape[1:])[:, c]

    s = s_ref[...]
    ys = []
    for c in range(nc):
        y_st = _bmm(chunk(r2_st, c), s, _BNT) + chunk(y0_st, c)
        s = _bmm(s, chunk(g_mat, c)) + chunk(h_mat, c)
        ys.append(y_st[:, :CHUNK] + y_st[:, CHUNK:])
    s_ref[...] = s
    y = jnp.concatenate([jnp.concatenate([yc[p] for yc in ys], axis=0) for p in range(PAIRS)], axis=1)

    mean = _head_sum(y, ones_h) * (1.0 / HEAD_DIM)
    yc = y - mean
    var = _head_sum(yc * yc, ones_h) * (1.0 / HEAD_DIM)
    yn = yc * lax.rsqrt(var + LNX_EPS) * lnw_ref[...] + lnb_ref[...]
    y_ref[...] = ((yn + bonus_ref[...].astype(F32)) * gate_ref[...].astype(F32)).astype(y_ref.dtype)


def _rwkv(ops, bg, wc, lnw, lnb, tt):
    b, t, _ = ops.shape
    col = lambda cb: pl.BlockSpec((None, tt, MIX_HALF), lambda bi, ti, cb=cb: (bi, ti, cb))
    vec = pl.BlockSpec((1, MIX_HALF), lambda bi, ti: (0, 0))
    return pl.pallas_call(
        functools.partial(_rwkv_kernel, tt=tt),
        grid=(b, t // tt),
        in_specs=[col(i) for i in range(5)] + [col(0), col(1)]
        + [pl.BlockSpec((None, tt // CHUNK, 1, MIX_HALF), lambda bi, ti: (bi, ti, 0, 0)), vec, vec],
        out_specs=pl.BlockSpec((None, tt, MIX_HALF), lambda bi, ti: (bi, ti, 0)),
        out_shape=jax.ShapeDtypeStruct((b, t, MIX_HALF), BF16),
        scratch_shapes=[pltpu.VMEM((PAIRS, LANES, LANES), F32)],
        compiler_params=_params(("arbitrary", "arbitrary")),
        name="rwkv7",
    )(*([ops] * 5), bg, bg, wc, lnw, lnb)


def _fox_attn_kernel(qa_ref, ka_ref, va_ref, og_ref, ow_ref, y_ref, m_ref, acc_ref, *, tq):
    lane = _iota((1, LANES), 1)
    ones_h = _head_ones()

    def softmax_pv(s, m_new, start):
        p = [jnp.concatenate([jnp.exp2(s[h][:, c:c + LANES] - m_new[h]) for c in range(0, tq, LANES)],
                             axis=1).astype(BF16) for h in range(2)]
        return [_dot(p[h], va_ref[h, pl.ds(start, tq), :]) for h in range(2)]

    def q_tile(i, carry):
        q0 = pl.multiple_of(i * tq, tq)
        q = [qa_ref[h, pl.ds(q0, tq), :] for h in range(2)]

        def scores(start):
            return [_dot(q[h], ka_ref[h, pl.ds(start, tq), :], _NT) for h in range(2)]

        half = tq // 2
        parts = [(h, r0, nk) for r0, nk in ((0, half), (half, tq)) for h in range(2)]
        s = [_dot(q[h][r0:r0 + half, :], ka_ref[h, pl.ds(q0, nk), :], _NT) for h, r0, nk in parts]
        s = [jnp.where(_iota((half, nk), 1) <= _iota((half, nk), 0) + r0, x, -jnp.inf)
             for x, (h, r0, nk) in zip(s, parts)]
        m_new = [jnp.broadcast_to(jnp.max(x, axis=1, keepdims=True), (half, LANES)) for x in s]
        p = [jnp.concatenate([jnp.exp2(x[:, c:c + LANES] - m) for c in range(0, x.shape[1], LANES)],
                             axis=1).astype(BF16) for x, m in zip(s, m_new)]
        pv = [_dot(x, va_ref[h, pl.ds(q0, nk), :]) for x, (h, r0, nk) in zip(p, parts)]
        for x, m, (h, r0, nk) in zip(pv, m_new, parts):
            acc_ref[h, r0:r0 + half, :] = x
            m_ref[h, r0:r0 + half, :] = m

        def step(j, c):
            start = pl.multiple_of(j * tq, tq)
            s = scores(start)
            m_old = [m_ref[h] for h in range(2)]
            m_new = [jnp.maximum(m_old[h], jnp.max(s[h], axis=1, keepdims=True)) for h in range(2)]
            pv = softmax_pv(s, m_new, start)
            for h in range(2):
                acc_ref[h] = jnp.exp2(m_old[h] - m_new[h]) * acc_ref[h] + pv[h]
                m_ref[h] = m_new[h]
            return c

        lax.fori_loop(0, i, step, 0)

        halves = []
        for h in range(2):
            acc = acc_ref[h]
            denom = jnp.sum(jnp.where(lane == HEAD_DIM * (1 - h), acc, 0.0), axis=1, keepdims=True)
            halves.append(acc / denom)
        o = jnp.where(lane < HEAD_DIM, halves[0], halves[1])
        ms = _head_sum(o * o, ones_h) * (1.0 / HEAD_DIM)
        gate = _sigmoid(og_ref[pl.ds(q0, tq), :].astype(F32))
        y_ref[pl.ds(q0, tq), :] = (o * lax.rsqrt(ms + NORM_EPS) * ow_ref[...] * gate).astype(y_ref.dtype)
        return carry

    lax.fori_loop(0, qa_ref.shape[1] // tq, q_tile, 0)


def _fox_attn(qa, ka, va, og, ow, tq):
    b, _, t, _ = qa.shape
    seq_spec = pl.BlockSpec((None, 2, t, LANES), lambda bi, hp: (bi, hp, 0, 0))
    col_spec = pl.BlockSpec((None, t, LANES), lambda bi, hp: (bi, 0, hp))
    return pl.pallas_call(
        functools.partial(_fox_attn_kernel, tq=tq),
        grid=(b, PAIRS),
        in_specs=[seq_spec, seq_spec, seq_spec, col_spec, pl.BlockSpec((1, LANES), lambda bi, hp: (0, 0))],
        out_specs=col_spec,
        out_shape=jax.ShapeDtypeStruct((b, t, MIX_HALF), BF16),
        scratch_shapes=[pltpu.VMEM((2, tq, LANES), F32), pltpu.VMEM((2, tq, LANES), F32)],
        compiler_params=_params(("arbitrary", "arbitrary")),
        name="fox_attn",
    )(qa, ka, va, og, ow)


def _ffn_kernel(x_ref, yr_ref, yf_ref, wo_ref, nf_ref, wu_ref, cw_ref, cb_ref, wd_ref, nfin_ref, o_ref,
                x2_ref, hid_ref, carry_ref, *, tm):
    @pl.when(pl.program_id(1) == 0)
    def _():
        carry_ref[...] = jnp.zeros_like(carry_ref)

    x2 = (x_ref[...] + _dot(yr_ref[...], wo_ref[:MIX_HALF, :]) + _dot(yf_ref[...], wo_ref[MIX_HALF:, :]))
    x2_ref[...] = x2
    ms = jnp.mean(x2 * x2, axis=-1, keepdims=True)
    h2 = (x2 * lax.rsqrt(ms + NORM_EPS) * nf_ref[...]).astype(BF16)

    def up(lo, w):
        return [_dot(h2, wu_ref[:, off + lo:off + lo + w]) for off in (0, D_FF)]

    def conv(u, cols):
        ext = jnp.concatenate([carry_ref[:, cols], u], axis=0)
        carry_ref[:, cols] = u[tm - 8:, :]
        u1 = pltpu.roll(ext, 1, axis=0)[8:, :]
        u2 = pltpu.roll(ext, 2, axis=0)[8:, :]
        cw = cw_ref[:, cols]
        return cw[0:1, :] * u2 + cw[1:2, :] * u1 + cw[2:3, :] * u + cb_ref[:, cols]

    def conv_glu(lo, w, u):
        gate = conv(u[0], slice(lo, lo + w))
        val = conv(u[1], slice(D_FF + lo, D_FF + lo + w))
        hid_ref[:, lo:lo + w] = (gate * _sigmoid(gate) * val).astype(BF16)

    chunks = [(lo, min(FF_CHUNK, D_FF - lo)) for lo in range(0, D_FF, FF_CHUNK)]
    u_prev = up(*chunks[0])
    for prev, cur in zip(chunks[:-1], chunks[1:]):
        u_next = up(*cur)
        conv_glu(*prev, u_prev)
        u_prev = u_next
    conv_glu(*chunks[-1], u_prev)

    xo = x2_ref[...] + _dot(hid_ref[...], wd_ref[...])
    ms = jnp.mean(xo * xo, axis=-1, keepdims=True)
    o_ref[...] = xo * lax.rsqrt(ms + NORM_EPS) * nfin_ref[...]


def _ffn(x3, yr, yf, wo, nf, wu, cw, cb, wd, nfin, tm):
    b, t, _ = x3.shape
    row = lambda w: pl.BlockSpec((None, tm, w), lambda bi, ti: (bi, ti, 0))
    return pl.pallas_call(
        functools.partial(_ffn_kernel, tm=tm),
        grid=(b, t // tm),
        in_specs=[row(D_MODEL), row(MIX_HALF), row(MIX_HALF)]
        + [_resident(a.shape) for a in (wo, nf, wu, cw, cb, wd, nfin)],
        out_specs=row(D_MODEL),
        out_shape=jax.ShapeDtypeStruct(x3.shape, F32),
        scratch_shapes=[pltpu.VMEM((tm, D_MODEL), F32), pltpu.VMEM((tm, D_FF), BF16),
                        pltpu.VMEM((8, 2 * D_FF), F32)],
        compiler_params=_params(("arbitrary", "arbitrary")),
        name="outproj_convffn",
    )(x3, yr, yf, wo, nf, wu, cw, cb, wd, nfin)


def _pack_w_in(w_in):
    fox = w_in[:, RWKV_COLS:]
    fl = fox[:, 4 * MIX_HALF:]
    pad = jnp.zeros((D_MODEL, FOX_Q_COL - RWKV_COLS - fl.shape[1]), w_in.dtype)
    return jnp.concatenate([w_in[:, :RWKV_COLS], fl, pad, fox[:, :4 * MIX_HALF]], axis=1).astype(BF16)


def _tile(n, pref):
    t = min(n, pref)
    assert n % t == 0, (n, t)
    return t


def kernel(x, norm_mix_w, w_in, rwkv_mu, rwkv_w0, rwkv_w2, rwkv_a0, rwkv_a2, rwkv_g2, rwkv_k_k, rwkv_k_a,
           rwkv_r_k, rwkv_lnx_w, rwkv_lnx_b, fox_f_bias, fox_q_norm_w, fox_k_norm_w, fox_o_norm_w, w_out,
           norm_ffn_w, ffn_w_up, ffn_conv_w, ffn_conv_b, ffn_w_down, norm_final_w):
    b, t, d = x.shape
    assert d == D_MODEL and norm_mix_w.shape[0] == 1 and t % CHUNK == 0
    row = lambda a: a.reshape(1, -1).astype(F32)
    twice = lambda a: jnp.tile(a.reshape(1, HEAD_DIM), (1, 2)).astype(F32)

    fb = jnp.zeros((1, LANES), F32).at[0, :N_HEADS].set(fox_f_bias[0])
    zeros = jnp.zeros((DECAY_LORA, MIX_HALF), F32)
    w2p = jnp.concatenate([rwkv_w2[0], zeros], axis=0).astype(BF16)
    a2p = jnp.concatenate([zeros, rwkv_a2[0]], axis=0).astype(BF16)
    rwkv_consts = [row(a[0]) for a in (rwkv_mu, rwkv_w0, rwkv_a0, rwkv_k_k, rwkv_k_a, rwkv_r_k)]
    rwkv_consts += [w2p, a2p, rwkv_g2[0].astype(BF16)]
    ops, bg, wc, og, qa, ka, va = _inproj(x, row(norm_mix_w[0]), _pack_w_in(w_in[0]), fb, twice(fox_q_norm_w[0]),
                                          twice(fox_k_norm_w[0]), rwkv_consts, _tile(t, TILE_INPROJ))

    y_rwkv = _rwkv(ops, bg, wc, row(rwkv_lnx_w[0]), row(rwkv_lnx_b[0]), _tile(t, TILE_RWKV))

    y_fox = _fox_attn(qa, ka, va, og, twice(fox_o_norm_w[0]), _tile(t, TILE_FOX_ATTN))

    out = _ffn(x, y_rwkv, y_fox, w_out[0].astype(BF16), row(norm_ffn_w[0]),
               ffn_w_up[0].astype(BF16), ffn_conv_w[0].astype(F32), row(ffn_conv_b[0]),
               ffn_w_down[0].astype(BF16), row(norm_final_w),
               _tile(t, TILE_FFN))
    return out.astype(x.dtype)
```

```python
import functools

import jax
import jax.numpy as jnp
import numpy as np
from jax import lax
from jax.experimental import pallas as pl
from jax.experimental.pallas import tpu as pltpu

F32 = jnp.float32
BF16 = jnp.bfloat16

D_MODEL = 1024
HEAD_DIM = 64
N_HEADS = 8
MIX_HALF = N_HEADS * HEAD_DIM
DECAY_LORA = 64
AAA_LORA = 64
GATE_LORA = 128
RWKV_COLS = 3 * MIX_HALF + DECAY_LORA + AAA_LORA + GATE_LORA
D_FF = 2816
NORM_EPS = 1e-6
LNX_EPS = 64e-5

LANES = 128
PAIRS = MIX_HALF // LANES
CHUNK = 64
FL_BLOCK = RWKV_COLS // LANES
FOX_Q_COL = 2048
FF_CHUNK = 512
CUMSUM_ROWS = 256
VMEM_LIMIT = 56 * 1024 * 1024
LOG2E = 1.4426950408889634

TILE_INPROJ = 512
TILE_RWKV = 512
TILE_FOX_ATTN = 512
TILE_FFN = 512


def _params(sem):
    return pltpu.CompilerParams(dimension_semantics=sem, vmem_limit_bytes=VMEM_LIMIT)


def _split3(x):
    hi = x.astype(BF16)
    r1 = x - hi.astype(F32)
    mid = r1.astype(BF16)
    lo = (r1 - mid.astype(F32)).astype(BF16)
    return hi, mid, lo


def _dot(a, b, dims=None):
    if dims is None:
        dims = (((a.ndim - 1,), (0,)), ((), ()))
    return lax.dot_general(a, b, dims, preferred_element_type=F32)


_NT = (((1,), (1,)), ((), ()))

_BNN = (((2,), (1,)), ((0,), (0,)))
_BNT = (((2,), (2,)), ((0,), (0,)))
_BTN = (((1,), (1,)), ((0,), (0,)))


def _bmm(a, b, dims=_BNN):
    return lax.dot_general(a.astype(BF16), b.astype(BF16), dims, preferred_element_type=F32)


def _dot_exact_lhs(lhs_bf16, x):
    hi, mid, lo = _split3(x)
    return _dot(lhs_bf16, hi) + _dot(lhs_bf16, mid) + _dot(lhs_bf16, lo)


def _iota(shape, dim):
    return lax.broadcasted_iota(jnp.int32, shape, dim)


def _head_ones():
    r = _iota((LANES, LANES), 0) // HEAD_DIM
    c = _iota((LANES, LANES), 1) // HEAD_DIM
    return jnp.where(r == c, 1.0, 0.0).astype(BF16)


def _head_sum(x, ones_h):
    xb = x.astype(BF16)
    return jnp.concatenate([_dot(xb[:, i:i + LANES], ones_h) for i in range(0, x.shape[1], LANES)], axis=1)


def _block_tri(n):
    r = _iota((n, n), 0)
    c = _iota((n, n), 1)
    return jnp.where((r // CHUNK == c // CHUNK) & (c <= r), 1.0, 0.0).astype(BF16)


def _sigmoid(z):
    return 1.0 / (1.0 + jnp.exp(-z))


def _softplus(z):
    return jnp.maximum(z, 0.0) + jnp.log1p(jnp.exp(-jnp.abs(z)))


def _resident(shape):
    return pl.BlockSpec(shape, lambda *_: (0,) * len(shape), pipeline_mode=pl.Buffered(1))


def _aug_tables():
    sel = np.zeros((LANES, N_HEADS * LANES), np.float32)
    for h in range(N_HEADS):
        a0 = h * LANES + HEAD_DIM * (1 - h % 2)
        for i in range(3):
            sel[i * N_HEADS + h, a0 + i] = 1.0
            sel[i * N_HEADS + h, a0 + 3 + i] = -1.0
    return jnp.asarray(sel, BF16)


def _inproj_kernel(x_ref, nw_ref, w_ref, fb_ref, qw_ref, kw_ref, sel_ref,
                   mu_ref, w0_ref, a0_ref, kkw_ref, kaw_ref, rkw_ref, w2_ref, a2_ref, g2_ref,
                   ops_ref, bg_ref, wc_ref, og_ref, qa_ref, ka_ref, va_ref, carry_ref, prev_ref, *, tm):
    @pl.when(pl.program_id(1) == 0)
    def _():
        carry_ref[...] = jnp.zeros_like(carry_ref)
        prev_ref[...] = jnp.zeros_like(prev_ref)

    x = x_ref[...]
    ms = jnp.mean(x * x, axis=-1, keepdims=True)
    h = (x * lax.rsqrt(ms + NORM_EPS) * nw_ref[...]).astype(BF16)
    half_w = MIX_HALF // 2
    proj = {}

    def project(name, lo, width=half_w):
        proj[name] = _dot(h, w_ref[:, lo:lo + width])

    def matmuls():
        project("fl", FL_BLOCK * LANES, LANES)
        for j, name in enumerate(("fq0", "fk0", "fv0")):
            project(name, FOX_Q_COL + j * MIX_HALF)
        yield
        for j, name in enumerate(("fq1", "fk1", "fv1")):
            project(name, FOX_Q_COL + j * MIX_HALF + half_w)
            yield
        project("wg", 3 * MIX_HALF)
        yield
        for j, name in ((1, "rk0"), (1, "rk1"), (0, "rr0"), (0, "rr1"), (2, "rv0"), (2, "rv1")):
            project(name, j * MIX_HALF + int(name[-1]) * half_w)
            yield
        og = FOX_Q_COL + 3 * MIX_HALF
        og_ref[:, :half_w] = _dot(h, w_ref[:, og:og + half_w]).astype(BF16)
        yield
        og_ref[:, half_w:] = _dot(h, w_ref[:, og + half_w:og + MIX_HALF]).astype(BF16)

    lane = _iota((1, LANES), 1)
    ones_h = _head_ones()

    def shifted(name, lo):
        p = proj.pop(name)
        cols = slice(lo, lo + p.shape[1])
        prev = jnp.where(_iota(p.shape, 0) == 0, prev_ref[7:8, cols], pltpu.roll(p, 1, axis=0))
        prev_ref[:, cols] = p[tm - 8:, :]
        return p + (prev - p) * mu_ref[:, cols]

    def prepare():
        logf = -_softplus(-(proj.pop("fl") + fb_ref[...]))
        blk = min(tm, CUMSUM_ROWS)
        tri = jnp.where(_iota((blk, blk), 1) <= _iota((blk, blk), 0), 1.0, 0.0).astype(BF16)
        run = carry_ref[0:1, :]
        c_blocks = []
        for r0 in range(0, tm, blk):
            cb = _dot_exact_lhs(tri, logf[r0:r0 + blk, :]) + run
            run = cb[blk - 1:blk, :]
            c_blocks.append(cb)
        carry_ref[...] = jnp.broadcast_to(run, carry_ref.shape)
        c = jnp.concatenate(c_blocks, axis=0)
        c_hi, c_mid, c_lo = (p.astype(F32) for p in _split3(c * LOG2E))
        packed = jnp.where(lane < N_HEADS, c_hi, jnp.where(lane < 2 * N_HEADS, pltpu.roll(c_mid, N_HEADS, axis=1),
                                                           pltpu.roll(c_lo, 2 * N_HEADS, axis=1)))
        aug = _dot(packed.astype(BF16), sel_ref[...])
        r2 = _iota((2 * LANES, 2 * LANES), 0) // HEAD_DIM
        c2 = _iota((2 * LANES, 2 * LANES), 1) // HEAD_DIM
        ones_qk = jnp.where(r2 == c2, 1.0, 0.0).astype(BF16)
        yield
        for pb in range(PAIRS):
            sl = slice((pb % 2) * LANES, (pb % 2 + 1) * LANES)
            q, k, v = (proj[name + str(pb // 2)][:, sl] for name in ("fq", "fk", "fv"))
            ss = _dot(jnp.concatenate([q * q, k * k], axis=1).astype(BF16), ones_qk) * (1.0 / HEAD_DIM)
            qn = q * lax.rsqrt(ss[:, :LANES] + NORM_EPS) * (qw_ref[...] * (HEAD_DIM ** -0.5 * LOG2E))
            kn = k * lax.rsqrt(ss[:, LANES:] + NORM_EPS) * kw_ref[...]
            for half in range(2):
                hd = 2 * pb + half
                own = (lane // HEAD_DIM) == half
                a0 = HEAD_DIM * (1 - half)
                z = aug[:, hd * LANES:(hd + 1) * LANES]
                c_q = (lane >= a0) & (lane < a0 + 3)
                c_k = (lane >= a0 + 3) & (lane < a0 + 6)
                qa = jnp.where(own, qn, jnp.where(c_q, z, jnp.where(c_k, 1.0, 0.0)))
                ka = jnp.where(own, kn, jnp.where(c_k, z, jnp.where(c_q, 1.0, 0.0)))
                va = jnp.where(own, v, jnp.where(lane == a0, 1.0, 0.0))
                qa_ref[hd] = qa.astype(BF16)
                ka_ref[hd] = ka.astype(BF16)
                va_ref[hd] = va.astype(BF16)
            yield

        wg = shifted("wg", 3 * MIX_HALF)
        tanh_wa = jnp.tanh(wg[:, :LANES]).astype(BF16)
        wa = wg[:, :LANES].astype(BF16)
        sig_gl = _sigmoid(wg[:, LANES:]).astype(BF16)
        tri = _block_tri(blk)
        st = []
        for j in range(2):
            cj = slice(j * half_w, (j + 1) * half_w)
            w = -_softplus(-(w0_ref[:, cj] + _dot(tanh_wa, w2_ref[:, cj]))) - 0.5
            logd = -jnp.exp(w)
            a_sig = _sigmoid(a0_ref[:, cj] + _dot(wa, a2_ref[:, cj]))
            bg_ref[:, MIX_HALF + j * half_w:MIX_HALF + (j + 1) * half_w] = _dot(sig_gl, g2_ref[:, cj]).astype(BF16)
            cs = jnp.concatenate([_dot_exact_lhs(tri, logd[r0:r0 + blk, :]) for r0 in range(0, tm, blk)], axis=0)
            w_incl = jnp.exp(cs)
            wc_ref[:, :, cj] = w_incl.reshape(tm // CHUNK, CHUNK, half_w)[:, CHUNK - 1:, :]
            st.append(dict(a_sig=a_sig, w_incl=w_incl, w_inv=jnp.exp(-cs), w_excl=jnp.exp(cs - logd)))
            yield
        group = lambda g, j: slice(g * MIX_HALF + j * half_w, g * MIX_HALF + (j + 1) * half_w)
        for j in range(2):
            k = shifted("rk%d" % j, MIX_HALF + j * half_w)
            kk = k * kkw_ref[:, group(0, j)]
            kk = kk * lax.rsqrt(jnp.maximum(_head_sum(kk * kk, ones_h), 1e-24))
            st[j]["k2"] = k * (1.0 + (st[j]["a_sig"] - 1.0) * kaw_ref[:, group(0, j)])
            ops_ref[:, group(0, j)] = ((-kk) * st[j]["w_excl"]).astype(BF16)
            ops_ref[:, group(2, j)] = (kk * st[j]["a_sig"] * st[j]["w_inv"]).astype(BF16)
            ops_ref[:, group(3, j)] = (st[j]["k2"] * st[j]["w_inv"]).astype(BF16)
            yield
        for j in range(2):
            st[j]["r"] = shifted("rr%d" % j, j * half_w)
            ops_ref[:, group(1, j)] = (st[j]["r"] * st[j]["w_incl"]).astype(BF16)
            yield
        for j in range(2):
            v = shifted("rv%d" % j, 2 * MIX_HALF + j * half_w)
            ops_ref[:, group(4, j)] = v.astype(BF16)
            bonus = _head_sum(st[j]["r"] * st[j]["k2"] * rkw_ref[:, group(0, j)], ones_h) * v
            bg_ref[:, group(0, j)] = bonus.astype(BF16)
            yield

    streams = [matmuls(), prepare()]
    while streams:
        for g in list(streams):
            if next(g, streams) is streams:
                streams.remove(g)


def _inproj(x3, norm_w, w_packed, f_bias, qw, kw, rwkv_consts, tm):
    b, t, _ = x3.shape
    row = lambda w: pl.BlockSpec((None, tm, w), lambda bi, ti: (bi, ti, 0))
    head_spec = pl.BlockSpec((None, N_HEADS, tm, LANES), lambda bi, ti: (bi, 0, ti, 0))
    head_shape = jax.ShapeDtypeStruct((b, N_HEADS, t, LANES), BF16)
    consts = (norm_w, w_packed, f_bias, qw, kw, _aug_tables()) + tuple(rwkv_consts)
    return pl.pallas_call(
        functools.partial(_inproj_kernel, tm=tm),
        grid=(b, t // tm),
        in_specs=[row(D_MODEL)] + [_resident(a.shape) for a in consts],
        out_specs=[row(5 * MIX_HALF), row(2 * MIX_HALF),
                   pl.BlockSpec((None, tm // CHUNK, 1, MIX_HALF), lambda bi, ti: (bi, ti, 0, 0)),
                   row(MIX_HALF), head_spec, head_spec, head_spec],
        out_shape=[jax.ShapeDtypeStruct((b, t, 5 * MIX_HALF), BF16), jax.ShapeDtypeStruct((b, t, 2 * MIX_HALF), BF16),
                   jax.ShapeDtypeStruct((b, t // CHUNK, 1, MIX_HALF), F32),
                   jax.ShapeDtypeStruct((b, t, MIX_HALF), BF16), head_shape, head_shape, head_shape],
        scratch_shapes=[pltpu.VMEM((8, LANES), F32), pltpu.VMEM((8, RWKV_COLS), F32)],
        compiler_params=_params(("arbitrary", "arbitrary")),
        name="inproj_foxprep",
    )(x3, *consts)


def _rwkv_kernel(a_ref, r_ref, b_ref, k_ref, v_ref, bonus_ref, gate_ref, wc_ref, lnw_ref, lnb_ref,
                 y_ref, s_ref, *, tt):
    @pl.when(pl.program_id(1) == 0)
    def _():
        s_ref[...] = jnp.zeros_like(s_ref)

    nc = tt // CHUNK
    ones_h = _head_ones()
    m0 = _iota((1, 1, LANES), 2) < HEAD_DIM

    def units(x_ref):
        out = []
        for p in range(PAIRS):
            xp = x_ref[:, p * LANES:(p + 1) * LANES].reshape(nc, CHUNK, LANES)
            zero = jnp.zeros_like(xp)
            out.append(jnp.concatenate([jnp.where(m0, xp, zero), jnp.where(m0, zero, xp)], axis=1))
        return jnp.concatenate(out, axis=0)

    a_st, r_st, b_st, k_st, v_st = (units(ref) for ref in (a_ref, r_ref, b_ref, k_ref, v_ref))
    wc = jnp.concatenate([wc_ref[:, :, p * LANES:(p + 1) * LANES] for p in range(PAIRS)], axis=0)

    ri = _iota((1, LANES, LANES), 1)
    ci = _iota((1, LANES, LANES), 2)
    same = (ri // CHUNK) == (ci // CHUNK)
    strict = same & (ci < ri)
    incl = same & (ci <= ri)
    eye = jnp.where(ri == ci, 1.0, 0.0).astype(F32)

    bk = jnp.concatenate([b_st, k_st], axis=1)
    sc = _bmm(jnp.concatenate([a_st, r_st], axis=1), bk, _BNT)
    l_ab = jnp.where(strict, sc[:, :LANES, :LANES], 0.0)
    l_ak = jnp.where(strict, sc[:, :LANES, LANES:], 0.0)
    m_rbk = jnp.concatenate([jnp.where(incl, sc[:, LANES:, :LANES], 0.0),
                             jnp.where(incl, sc[:, LANES:, LANES:], 0.0)], axis=2).astype(BF16)

    t_inv = eye + l_ab
    lp = l_ab.astype(BF16)
    lp = _bmm(lp, lp).astype(BF16)
    for _ in range(CHUNK.bit_length() - 3):
        both = _bmm(jnp.concatenate([lp, t_inv.astype(BF16)], axis=1), lp)
        t_inv = t_inv + both[:, LANES:]
        lp = both[:, :LANES].astype(BF16)
    t_inv = t_inv + _bmm(t_inv, lp)

    lakv = _bmm(l_ak, v_st)
    au = _bmm(t_inv, jnp.concatenate([a_st, lakv.astype(BF16)], axis=2))
    au_b = au.astype(BF16)
    zv = jnp.concatenate([jnp.zeros_like(v_st), v_st], axis=2)
    ry = _bmm(m_rbk, jnp.concatenate([au_b, zv], axis=1))
    r2_st = r_st.astype(F32) + ry[:, :, :LANES]
    y0_st = ry[:, :, LANES:]
    g_mat = (eye + _bmm(au_b[:, :, :LANES], b_st, _BTN)) * wc
    h_mat = _bmm(jnp.concatenate([au_b[:, :, LANES:], v_st], axis=1), bk, _BTN) * wc

    def chunk(x, c):
        return x.reshape((PAIRS, nc) + x.shape[1:])[:, c]

    s = s_ref[...]
    ys = []
    for c in range(nc):
        y_st = _bmm(chunk(r2_st, c), s, _BNT) + chunk(y0_st, c)
        s = _bmm(s, chunk(g_mat, c)) + chunk(h_mat, c)
        ys.append(y_st[:, :CHUNK] + y_st[:, CHUNK:])
    s_ref[...] = s
    y = jnp.concatenate([jnp.concatenate([yc[p] for yc in ys], axis=0) for p in range(PAIRS)], axis=1)

    mean = _head_sum(y, ones_h) * (1.0 / HEAD_DIM)
    yc = y - mean
    var = _head_sum(yc * yc, ones_h) * (1.0 / HEAD_DIM)
    yn = yc * lax.rsqrt(var + LNX_EPS) * lnw_ref[...] + lnb_ref[...]
    y_ref[...] = ((yn + bonus_ref[...].astype(F32)) * gate_ref[...].astype(F32)).astype(y_ref.dtype)


def _rwkv(ops, bg, wc, lnw, lnb, tt):
    b, t, _ = ops.shape
    col = lambda cb: pl.BlockSpec((None, tt, MIX_HALF), lambda bi, ti, cb=cb: (bi, ti, cb))
    vec = pl.BlockSpec((1, MIX_HALF), lambda bi, ti: (0, 0))
    return pl.pallas_call(
        functools.partial(_rwkv_kernel, tt=tt),
        grid=(b, t // tt),
        in_specs=[col(i) for i in range(5)] + [col(0), col(1)]
        + [pl.BlockSpec((None, tt // CHUNK, 1, MIX_HALF), lambda bi, ti: (bi, ti, 0, 0)), vec, vec],
        out_specs=pl.BlockSpec((None, tt, MIX_HALF), lambda bi, ti: (bi, ti, 0)),
        out_shape=jax.ShapeDtypeStruct((b, t, MIX_HALF), BF16),
        scratch_shapes=[pltpu.VMEM((PAIRS, LANES, LANES), F32)],
        compiler_params=_params(("arbitrary", "arbitrary")),
        name="rwkv7",
    )(*([ops] * 5), bg, bg, wc, lnw, lnb)


def _fox_attn_kernel(qa_ref, ka_ref, va_ref, og_ref, ow_ref, y_ref, m_ref, acc_ref, *, tq):
    lane = _iota((1, LANES), 1)
    ones_h = _head_ones()

    def q_tile(i, carry):
        q0 = pl.multiple_of(i * tq, tq)
        q = [qa_ref[h, pl.ds(q0, tq), :] for h in range(2)]

        half = tq // 2
        parts = [(h, r0, nk) for r0, nk in ((0, half), (half, tq)) for h in range(2)]
        s = [_dot(q[h][r0:r0 + half, :], ka_ref[h, pl.ds(q0, nk), :], _NT) for h, r0, nk in parts]
        s = [jnp.where(_iota((half, nk), 1) <= _iota((half, nk), 0) + r0, x, -jnp.inf)
             for x, (h, r0, nk) in zip(s, parts)]
        m_new = [jnp.broadcast_to(jnp.max(x, axis=1, keepdims=True), (half, LANES)) for x in s]
        p = [jnp.concatenate([jnp.exp2(x[:, c:c + LANES] - m) for c in range(0, x.shape[1], LANES)],
                             axis=1).astype(BF16) for x, m in zip(s, m_new)]
        pv = [_dot(x, va_ref[h, pl.ds(q0, nk), :]) for x, (h, r0, nk) in zip(p, parts)]
        for x, m, (h, r0, nk) in zip(pv, m_new, parts):
            acc_ref[h, r0:r0 + half, :] = x
            m_ref[h, r0:r0 + half, :] = m

        def step(start, nk):
            s = [_dot(q[h], ka_ref[h, pl.ds(start, nk), :], _NT) for h in range(2)]
            m_old = [m_ref[h] for h in range(2)]
            m_new = [jnp.maximum(m_old[h], jnp.max(s[h], axis=1, keepdims=True)) for h in range(2)]
            p = [jnp.concatenate([jnp.exp2(s[h][:, c:c + LANES] - m_new[h]) for c in range(0, nk, LANES)],
                                 axis=1).astype(BF16) for h in range(2)]
            pv = [_dot(p[h], va_ref[h, pl.ds(start, nk), :]) for h in range(2)]
            for h in range(2):
                acc_ref[h] = jnp.exp2(m_old[h] - m_new[h]) * acc_ref[h] + pv[h]
                m_ref[h] = m_new[h]

        def pair_step(jj, c):
            step(pl.multiple_of(jj * (2 * tq), 2 * tq), 2 * tq)
            return c

        lax.fori_loop(0, i // 2, pair_step, 0)

        @pl.when(i % 2 == 1)
        def _():
            step(q0 - tq, tq)

        halves = []
        for h in range(2):
            acc = acc_ref[h]
            denom = jnp.sum(jnp.where(lane == HEAD_DIM * (1 - h), acc, 0.0), axis=1, keepdims=True)
            halves.append(acc / denom)
        o = jnp.where(lane < HEAD_DIM, halves[0], halves[1])
        ms = _head_sum(o * o, ones_h) * (1.0 / HEAD_DIM)
        gate = _sigmoid(og_ref[pl.ds(q0, tq), :].astype(F32))
        y_ref[pl.ds(q0, tq), :] = (o * lax.rsqrt(ms + NORM_EPS) * ow_ref[...] * gate).astype(y_ref.dtype)
        return carry

    lax.fori_loop(0, qa_ref.shape[1] // tq, q_tile, 0)


def _fox_attn(qa, ka, va, og, ow, tq):
    b, _, t, _ = qa.shape
    seq_spec = pl.BlockSpec((None, 2, t, LANES), lambda bi, hp: (bi, hp, 0, 0))
    col_spec = pl.BlockSpec((None, t, LANES), lambda bi, hp: (bi, 0, hp))
    return pl.pallas_call(
        functools.partial(_fox_attn_kernel, tq=tq),
        grid=(b, PAIRS),
        in_specs=[seq_spec, seq_spec, seq_spec, col_spec, pl.BlockSpec((1, LANES), lambda bi, hp: (0, 0))],
        out_specs=col_spec,
        out_shape=jax.ShapeDtypeStruct((b, t, MIX_HALF), BF16),
        scratch_shapes=[pltpu.VMEM((2, tq, LANES), F32), pltpu.VMEM((2, tq, LANES), F32)],
        compiler_params=_params(("arbitrary", "arbitrary")),
        name="fox_attn",
    )(qa, ka, va, og, ow)


def _ffn_kernel(x_ref, yr_ref, yf_ref, wo_ref, nf_ref, wu_ref, cw_ref, cb_ref, wd_ref, nfin_ref, o_ref,
                x2_ref, hid_ref, carry_ref, *, tm):
    @pl.when(pl.program_id(1) == 0)
    def _():
        carry_ref[...] = jnp.zeros_like(carry_ref)

    x2 = (x_ref[...] + _dot(yr_ref[...], wo_ref[:MIX_HALF, :]) + _dot(yf_ref[...], wo_ref[MIX_HALF:, :]))
    x2_ref[...] = x2
    ms = jnp.mean(x2 * x2, axis=-1, keepdims=True)
    h2 = (x2 * lax.rsqrt(ms + NORM_EPS) * nf_ref[...]).astype(BF16)

    def up(lo, w):
        return [_dot(h2, wu_ref[:, off + lo:off + lo + w]) for off in (0, D_FF)]

    def conv(u, cols):
        ext = jnp.concatenate([carry_ref[:, cols], u], axis=0)
        carry_ref[:, cols] = u[tm - 8:, :]
        u1 = pltpu.roll(ext, 1, axis=0)[8:, :]
        u2 = pltpu.roll(ext, 2, axis=0)[8:, :]
        cw = cw_ref[:, cols]
        return cw[0:1, :] * u2 + cw[1:2, :] * u1 + cw[2:3, :] * u + cb_ref[:, cols]

    def conv_glu(lo, w, u):
        gate = conv(u[0], slice(lo, lo + w))
        val = conv(u[1], slice(D_FF + lo, D_FF + lo + w))
        hid_ref[:, lo:lo + w] = (gate * _sigmoid(gate) * val).astype(BF16)

    chunks = [(lo, min(FF_CHUNK, D_FF - lo)) for lo in range(0, D_FF, FF_CHUNK)]
    u_prev = up(*chunks[0])
    for prev, cur in zip(chunks[:-1], chunks[1:]):
        u_next = up(*cur)
        conv_glu(*prev, u_prev)
        u_prev = u_next
    conv_glu(*chunks[-1], u_prev)

    xo = x2_ref[...] + _dot(hid_ref[...], wd_ref[...])
    ms = jnp.mean(xo * xo, axis=-1, keepdims=True)
    o_ref[...] = xo * lax.rsqrt(ms + NORM_EPS) * nfin_ref[...]


def _ffn(x3, yr, yf, wo, nf, wu, cw, cb, wd, nfin, tm):
    b, t, _ = x3.shape
    row = lambda w: pl.BlockSpec((None, tm, w), lambda bi, ti: (bi, ti, 0))
    return pl.pallas_call(
        functools.partial(_ffn_kernel, tm=tm),
        grid=(b, t // tm),
        in_specs=[row(D_MODEL), row(MIX_HALF), row(MIX_HALF)]
        + [_resident(a.shape) for a in (wo, nf, wu, cw, cb, wd, nfin)],
        out_specs=row(D_MODEL),
        out_shape=jax.ShapeDtypeStruct(x3.shape, F32),
        scratch_shapes=[pltpu.VMEM((tm, D_MODEL), F32), pltpu.VMEM((tm, D_FF), BF16),
                        pltpu.VMEM((8, 2 * D_FF), F32)],
        compiler_params=_params(("arbitrary", "arbitrary")),
        name="outproj_convffn",
    )(x3, yr, yf, wo, nf, wu, cw, cb, wd, nfin)


def _pack_w_in(w_in):
    fox = w_in[:, RWKV_COLS:]
    fl = fox[:, 4 * MIX_HALF:]
    pad = jnp.zeros((D_MODEL, FOX_Q_COL - RWKV_COLS - fl.shape[1]), w_in.dtype)
    return jnp.concatenate([w_in[:, :RWKV_COLS], fl, pad, fox[:, :4 * MIX_HALF]], axis=1).astype(BF16)


def _tile(n, pref):
    t = min(n, pref)
    assert n % t == 0, (n, t)
    return t


def kernel(x, norm_mix_w, w_in, rwkv_mu, rwkv_w0, rwkv_w2, rwkv_a0, rwkv_a2, rwkv_g2, rwkv_k_k, rwkv_k_a,
           rwkv_r_k, rwkv_lnx_w, rwkv_lnx_b, fox_f_bias, fox_q_norm_w, fox_k_norm_w, fox_o_norm_w, w_out,
           norm_ffn_w, ffn_w_up, ffn_conv_w, ffn_conv_b, ffn_w_down, norm_final_w):
    b, t, d = x.shape
    assert d == D_MODEL and norm_mix_w.shape[0] == 1 and t % CHUNK == 0
    row = lambda a: a.reshape(1, -1).astype(F32)
    twice = lambda a: jnp.tile(a.reshape(1, HEAD_DIM), (1, 2)).astype(F32)

    fb = jnp.zeros((1, LANES), F32).at[0, :N_HEADS].set(fox_f_bias[0])
    zeros = jnp.zeros((DECAY_LORA, MIX_HALF), F32)
    w2p = jnp.concatenate([rwkv_w2[0], zeros], axis=0).astype(BF16)
    a2p = jnp.concatenate([zeros, rwkv_a2[0]], axis=0).astype(BF16)
    rwkv_consts = [row(a[0]) for a in (rwkv_mu, rwkv_w0, rwkv_a0, rwkv_k_k, rwkv_k_a, rwkv_r_k)]
    rwkv_consts += [w2p, a2p, rwkv_g2[0].astype(BF16)]
    ops, bg, wc, og, qa, ka, va = _inproj(x, row(norm_mix_w[0]), _pack_w_in(w_in[0]), fb, twice(fox_q_norm_w[0]),
                                          twice(fox_k_norm_w[0]), rwkv_consts, _tile(t, TILE_INPROJ))

    y_rwkv = _rwkv(ops, bg, wc, row(rwkv_lnx_w[0]), row(rwkv_lnx_b[0]), _tile(t, TILE_RWKV))

    y_fox = _fox_attn(qa, ka, va, og, twice(fox_o_norm_w[0]), _tile(t, TILE_FOX_ATTN))

    out = _ffn(x, y_rwkv, y_fox, w_out[0].astype(BF16), row(norm_ffn_w[0]),
               ffn_w_up[0].astype(BF16), ffn_conv_w[0].astype(F32), row(ffn_conv_b[0]),
               ffn_w_down[0].astype(BF16), row(norm_final_w),
               _tile(t, TILE_FFN))
    return out.astype(x.dtype)
```

```python
import functools

import jax
import jax.numpy as jnp
import numpy as np
from jax import lax
from jax.experimental import pallas as pl
from jax.experimental.pallas import tpu as pltpu

F32 = jnp.float32
BF16 = jnp.bfloat16

D_MODEL = 1024
HEAD_DIM = 64
N_HEADS = 8
MIX_HALF = N_HEADS * HEAD_DIM
DECAY_LORA = 64
AAA_LORA = 64
GATE_LORA = 128
RWKV_COLS = 3 * MIX_HALF + DECAY_LORA + AAA_LORA + GATE_LORA
D_FF = 2816
NORM_EPS = 1e-6
LNX_EPS = 64e-5

LANES = 128
PAIRS = MIX_HALF // LANES
CHUNK = 64
FL_BLOCK = RWKV_COLS // LANES
FOX_Q_COL = 2048
FF_CHUNK = 512
CUMSUM_ROWS = 256
VMEM_LIMIT = 56 * 1024 * 1024
LOG2E = 1.4426950408889634

TILE_INPROJ = 512
TILE_RWKV = 512
TILE_FOX_ATTN = 512
FOX_KEY_TILES = 4
TILE_FFN = 512


def _params(sem):
    return pltpu.CompilerParams(dimension_semantics=sem, vmem_limit_bytes=VMEM_LIMIT)


def _split3(x):
    hi = x.astype(BF16)
    r1 = x - hi.astype(F32)
    mid = r1.astype(BF16)
    lo = (r1 - mid.astype(F32)).astype(BF16)
    return hi, mid, lo


def _dot(a, b, dims=None):
    if dims is None:
        dims = (((a.ndim - 1,), (0,)), ((), ()))
    return lax.dot_general(a, b, dims, preferred_element_type=F32)


_NT = (((1,), (1,)), ((), ()))

_BNN = (((2,), (1,)), ((0,), (0,)))
_BNT = (((2,), (2,)), ((0,), (0,)))
_BTN = (((1,), (1,)), ((0,), (0,)))


def _bmm(a, b, dims=_BNN):
    return lax.dot_general(a.astype(BF16), b.astype(BF16), dims, preferred_element_type=F32)


def _dot_exact_lhs(lhs_bf16, x):
    hi, mid, lo = _split3(x)
    return _dot(lhs_bf16, hi) + _dot(lhs_bf16, mid) + _dot(lhs_bf16, lo)


def _iota(shape, dim):
    return lax.broadcasted_iota(jnp.int32, shape, dim)


def _head_ones():
    r = _iota((LANES, LANES), 0) // HEAD_DIM
    c = _iota((LANES, LANES), 1) // HEAD_DIM
    return jnp.where(r == c, 1.0, 0.0).astype(BF16)


def _head_sum(x, ones_h):
    xb = x.astype(BF16)
    return jnp.concatenate([_dot(xb[:, i:i + LANES], ones_h) for i in range(0, x.shape[1], LANES)], axis=1)


def _block_tri(n):
    r = _iota((n, n), 0)
    c = _iota((n, n), 1)
    return jnp.where((r // CHUNK == c // CHUNK) & (c <= r), 1.0, 0.0).astype(BF16)


def _sigmoid(z):
    return 1.0 / (1.0 + jnp.exp(-z))


def _softplus(z):
    return jnp.maximum(z, 0.0) + jnp.log1p(jnp.exp(-jnp.abs(z)))


def _resident(shape):
    return pl.BlockSpec(shape, lambda *_: (0,) * len(shape), pipeline_mode=pl.Buffered(1))


def _aug_tables():
    sel = np.zeros((LANES, N_HEADS * LANES), np.float32)
    for h in range(N_HEADS):
        a0 = h * LANES + HEAD_DIM * (1 - h % 2)
        for i in range(3):
            sel[i * N_HEADS + h, a0 + i] = 1.0
            sel[i * N_HEADS + h, a0 + 3 + i] = -1.0
    return jnp.asarray(sel, BF16)


def _inproj_kernel(x_ref, nw_ref, w_ref, fb_ref, qw_ref, kw_ref, sel_ref,
                   mu_ref, w0_ref, a0_ref, kkw_ref, kaw_ref, rkw_ref, w2_ref, a2_ref, g2_ref,
                   ops_ref, bg_ref, wc_ref, og_ref, qa_ref, ka_ref, va_ref, carry_ref, prev_ref, *, tm):
    @pl.when(pl.program_id(1) == 0)
    def _():
        carry_ref[...] = jnp.zeros_like(carry_ref)
        prev_ref[...] = jnp.zeros_like(prev_ref)

    x = x_ref[...]
    ms = jnp.mean(x * x, axis=-1, keepdims=True)
    h = (x * lax.rsqrt(ms + NORM_EPS) * nw_ref[...]).astype(BF16)
    half_w = MIX_HALF // 2
    proj = {}

    def project(name, lo, width=half_w):
        proj[name] = _dot(h, w_ref[:, lo:lo + width])

    def matmuls():
        project("fl", FL_BLOCK * LANES, LANES)
        for j, name in enumerate(("fq0", "fk0", "fv0")):
            project(name, FOX_Q_COL + j * MIX_HALF)
        yield
        for j, name in enumerate(("fq1", "fk1", "fv1")):
            project(name, FOX_Q_COL + j * MIX_HALF + half_w)
            yield
        project("wg", 3 * MIX_HALF)
        yield
        for j, name in ((1, "rk0"), (1, "rk1"), (0, "rr0"), (0, "rr1"), (2, "rv0"), (2, "rv1")):
            project(name, j * MIX_HALF + int(name[-1]) * half_w)
            yield
        og = FOX_Q_COL + 3 * MIX_HALF
        og_ref[:, :half_w] = _dot(h, w_ref[:, og:og + half_w]).astype(BF16)
        yield
        og_ref[:, half_w:] = _dot(h, w_ref[:, og + half_w:og + MIX_HALF]).astype(BF16)

    lane = _iota((1, LANES), 1)
    ones_h = _head_ones()

    def shifted(name, lo):
        p = proj.pop(name)
        cols = slice(lo, lo + p.shape[1])
        prev = jnp.where(_iota(p.shape, 0) == 0, prev_ref[7:8, cols], pltpu.roll(p, 1, axis=0))
        prev_ref[:, cols] = p[tm - 8:, :]
        return p + (prev - p) * mu_ref[:, cols]

    def prepare():
        logf = -_softplus(-(proj.pop("fl") + fb_ref[...]))
        blk = min(tm, CUMSUM_ROWS)
        tri = jnp.where(_iota((blk, blk), 1) <= _iota((blk, blk), 0), 1.0, 0.0).astype(BF16)
        run = carry_ref[0:1, :]
        c_blocks = []
        for r0 in range(0, tm, blk):
            cb = _dot_exact_lhs(tri, logf[r0:r0 + blk, :]) + run
            run = cb[blk - 1:blk, :]
            c_blocks.append(cb)
        carry_ref[...] = jnp.broadcast_to(run, carry_ref.shape)
        c = jnp.concatenate(c_blocks, axis=0)
        c_hi, c_mid, c_lo = (p.astype(F32) for p in _split3(c * LOG2E))
        packed = jnp.where(lane < N_HEADS, c_hi, jnp.where(lane < 2 * N_HEADS, pltpu.roll(c_mid, N_HEADS, axis=1),
                                                           pltpu.roll(c_lo, 2 * N_HEADS, axis=1)))
        aug = _dot(packed.astype(BF16), sel_ref[...])
        r2 = _iota((2 * LANES, 2 * LANES), 0) // HEAD_DIM
        c2 = _iota((2 * LANES, 2 * LANES), 1) // HEAD_DIM
        ones_qk = jnp.where(r2 == c2, 1.0, 0.0).astype(BF16)
        yield
        for pb in range(PAIRS):
            sl = slice((pb % 2) * LANES, (pb % 2 + 1) * LANES)
            q, k, v = (proj[name + str(pb // 2)][:, sl] for name in ("fq", "fk", "fv"))
            ss = _dot(jnp.concatenate([q * q, k * k], axis=1).astype(BF16), ones_qk) * (1.0 / HEAD_DIM)
            qn = q * lax.rsqrt(ss[:, :LANES] + NORM_EPS) * (qw_ref[...] * (HEAD_DIM ** -0.5 * LOG2E))
            kn = k * lax.rsqrt(ss[:, LANES:] + NORM_EPS) * kw_ref[...]
            for half in range(2):
                hd = 2 * pb + half
                own = (lane // HEAD_DIM) == half
                a0 = HEAD_DIM * (1 - half)
                z = aug[:, hd * LANES:(hd + 1) * LANES]
                c_q = (lane >= a0) & (lane < a0 + 3)
                c_k = (lane >= a0 + 3) & (lane < a0 + 6)
                qa = jnp.where(own, qn, jnp.where(c_q, z, jnp.where(c_k, 1.0, 0.0)))
                ka = jnp.where(own, kn, jnp.where(c_k, z, jnp.where(c_q, 1.0, 0.0)))
                va = jnp.where(own, v, jnp.where(lane == a0, 1.0, 0.0))
                qa_ref[hd] = qa.astype(BF16)
                ka_ref[hd] = ka.astype(BF16)
                va_ref[hd] = va.astype(BF16)
            yield

        wg = shifted("wg", 3 * MIX_HALF)
        tanh_wa = jnp.tanh(wg[:, :LANES]).astype(BF16)
        wa = wg[:, :LANES].astype(BF16)
        sig_gl = _sigmoid(wg[:, LANES:]).astype(BF16)
        tri = _block_tri(blk)
        st = []
        for j in range(2):
            cj = slice(j * half_w, (j + 1) * half_w)
            w = -_softplus(-(w0_ref[:, cj] + _dot(tanh_wa, w2_ref[:, cj]))) - 0.5
            logd = -jnp.exp(w)
            a_sig = _sigmoid(a0_ref[:, cj] + _dot(wa, a2_ref[:, cj]))
            bg_ref[:, MIX_HALF + j * half_w:MIX_HALF + (j + 1) * half_w] = _dot(sig_gl, g2_ref[:, cj]).astype(BF16)
            cs = jnp.concatenate([_dot_exact_lhs(tri, logd[r0:r0 + blk, :]) for r0 in range(0, tm, blk)], axis=0)
            w_incl = jnp.exp(cs)
            wc_ref[:, :, cj] = w_incl.reshape(tm // CHUNK, CHUNK, half_w)[:, CHUNK - 1:, :]
            st.append(dict(a_sig=a_sig, w_incl=w_incl, w_inv=jnp.exp(-cs), w_excl=jnp.exp(cs - logd)))
            yield
        group = lambda g, j: slice(g * MIX_HALF + j * half_w, g * MIX_HALF + (j + 1) * half_w)
        for j in range(2):
            k = shifted("rk%d" % j, MIX_HALF + j * half_w)
            kk = k * kkw_ref[:, group(0, j)]
            kk = kk * lax.rsqrt(jnp.maximum(_head_sum(kk * kk, ones_h), 1e-24))
            st[j]["k2"] = k * (1.0 + (st[j]["a_sig"] - 1.0) * kaw_ref[:, group(0, j)])
            ops_ref[:, group(0, j)] = ((-kk) * st[j]["w_excl"]).astype(BF16)
            ops_ref[:, group(2, j)] = (kk * st[j]["a_sig"] * st[j]["w_inv"]).astype(BF16)
            ops_ref[:, group(3, j)] = (st[j]["k2"] * st[j]["w_inv"]).astype(BF16)
            yield
        for j in range(2):
            st[j]["r"] = shifted("rr%d" % j, j * half_w)
            ops_ref[:, group(1, j)] = (st[j]["r"] * st[j]["w_incl"]).astype(BF16)
            yield
        for j in range(2):
            v = shifted("rv%d" % j, 2 * MIX_HALF + j * half_w)
            ops_ref[:, group(4, j)] = v.astype(BF16)
            bonus = _head_sum(st[j]["r"] * st[j]["k2"] * rkw_ref[:, group(0, j)], ones_h) * v
            bg_ref[:, group(0, j)] = bonus.astype(BF16)
            yield

    streams = [matmuls(), prepare()]
    while streams:
        for g in list(streams):
            if next(g, streams) is streams:
                streams.remove(g)


def _inproj(x3, norm_w, w_packed, f_bias, qw, kw, rwkv_consts, tm):
    b, t, _ = x3.shape
    row = lambda w: pl.BlockSpec((None, tm, w), lambda bi, ti: (bi, ti, 0))
    head_spec = pl.BlockSpec((None, N_HEADS, tm, LANES), lambda bi, ti: (bi, 0, ti, 0))
    head_shape = jax.ShapeDtypeStruct((b, N_HEADS, t, LANES), BF16)
    consts = (norm_w, w_packed, f_bias, qw, kw, _aug_tables()) + tuple(rwkv_consts)
    return pl.pallas_call(
        functools.partial(_inproj_kernel, tm=tm),
        grid=(b, t // tm),
        in_specs=[row(D_MODEL)] + [_resident(a.shape) for a in consts],
        out_specs=[row(5 * MIX_HALF), row(2 * MIX_HALF),
                   pl.BlockSpec((None, tm // CHUNK, 1, MIX_HALF), lambda bi, ti: (bi, ti, 0, 0)),
                   row(MIX_HALF), head_spec, head_spec, head_spec],
        out_shape=[jax.ShapeDtypeStruct((b, t, 5 * MIX_HALF), BF16), jax.ShapeDtypeStruct((b, t, 2 * MIX_HALF), BF16),
                   jax.ShapeDtypeStruct((b, t // CHUNK, 1, MIX_HALF), F32),
                   jax.ShapeDtypeStruct((b, t, MIX_HALF), BF16), head_shape, head_shape, head_shape],
        scratch_shapes=[pltpu.VMEM((8, LANES), F32), pltpu.VMEM((8, RWKV_COLS), F32)],
        compiler_params=_params(("arbitrary", "arbitrary")),
        name="inproj_foxprep",
    )(x3, *consts)


def _rwkv_kernel(a_ref, r_ref, b_ref, k_ref, v_ref, bonus_ref, gate_ref, wc_ref, lnw_ref, lnb_ref,
                 y_ref, s_ref, *, tt):
    @pl.when(pl.program_id(1) == 0)
    def _():
        s_ref[...] = jnp.zeros_like(s_ref)

    nc = tt // CHUNK
    ones_h = _head_ones()
    m0 = _iota((1, 1, LANES), 2) < HEAD_DIM

    def units(x_ref):
        out = []
        for p in range(PAIRS):
            xp = x_ref[:, p * LANES:(p + 1) * LANES].reshape(nc, CHUNK, LANES)
            zero = jnp.zeros_like(xp)
            out.append(jnp.concatenate([jnp.where(m0, xp, zero), jnp.where(m0, zero, xp)], axis=1))
        return jnp.concatenate(out, axis=0)

    a_st, r_st, b_st, k_st, v_st = (units(ref) for ref in (a_ref, r_ref, b_ref, k_ref, v_ref))
    wc = jnp.concatenate([wc_ref[:, :, p * LANES:(p + 1) * LANES] for p in range(PAIRS)], axis=0)

    ri = _iota((1, LANES, LANES), 1)
    ci = _iota((1, LANES, LANES), 2)
    same = (ri // CHUNK) == (ci // CHUNK)
    strict = same & (ci < ri)
    incl = same & (ci <= ri)
    eye = jnp.where(ri == ci, 1.0, 0.0).astype(F32)

    bk = jnp.concatenate([b_st, k_st], axis=1)
    sc = _bmm(jnp.concatenate([a_st, r_st], axis=1), bk, _BNT)
    l_ab = jnp.where(strict, sc[:, :LANES, :LANES], 0.0)
    l_ak = jnp.where(strict, sc[:, :LANES, LANES:], 0.0)
    m_rbk = jnp.concatenate([jnp.where(incl, sc[:, LANES:, :LANES], 0.0),
                             jnp.where(incl, sc[:, LANES:, LANES:], 0.0)], axis=2).astype(BF16)

    t_inv = eye + l_ab
    lp = l_ab.astype(BF16)
    lp = _bmm(lp, lp).astype(BF16)
    for _ in range(CHUNK.bit_length() - 3):
        both = _bmm(jnp.concatenate([lp, t_inv.astype(BF16)], axis=1), lp)
        t_inv = t_inv + both[:, LANES:]
        lp = both[:, :LANES].astype(BF16)
    t_inv = t_inv + _bmm(t_inv, lp)

    lakv = _bmm(l_ak, v_st)
    au = _bmm(t_inv, jnp.concatenate([a_st, lakv.astype(BF16)], axis=2))
    au_b = au.astype(BF16)
    zv = jnp.concatenate([jnp.zeros_like(v_st), v_st], axis=2)
    ry = _bmm(m_rbk, jnp.concatenate([au_b, zv], axis=1))
    r2_st = r_st.astype(F32) + ry[:, :, :LANES]
    y0_st = ry[:, :, LANES:]
    g_mat = (eye + _bmm(au_b[:, :, :LANES], b_st, _BTN)) * wc
    h_mat = _bmm(jnp.concatenate([au_b[:, :, LANES:], v_st], axis=1), bk, _BTN) * wc

    def chunk(x, c):
        return x.reshape((PAIRS, nc) + x.shape[1:])[:, c]

    s = s_ref[...]
    ys = []
    for c in range(nc):
        y_st = _bmm(chunk(r2_st, c), s, _BNT) + chunk(y0_st, c)
        s = _bmm(s, chunk(g_mat, c)) + chunk(h_mat, c)
        ys.append(y_st[:, :CHUNK] + y_st[:, CHUNK:])
    s_ref[...] = s
    y = jnp.concatenate([jnp.concatenate([yc[p] for yc in ys], axis=0) for p in range(PAIRS)], axis=1)

    mean = _head_sum(y, ones_h) * (1.0 / HEAD_DIM)
    yc = y - mean
    var = _head_sum(yc * yc, ones_h) * (1.0 / HEAD_DIM)
    yn = yc * lax.rsqrt(var + LNX_EPS) * lnw_ref[...] + lnb_ref[...]
    y_ref[...] = ((yn + bonus_ref[...].astype(F32)) * gate_ref[...].astype(F32)).astype(y_ref.dtype)


def _rwkv(ops, bg, wc, lnw, lnb, tt):
    b, t, _ = ops.shape
    col = lambda cb: pl.BlockSpec((None, tt, MIX_HALF), lambda bi, ti, cb=cb: (bi, ti, cb))
    vec = pl.BlockSpec((1, MIX_HALF), lambda bi, ti: (0, 0))
    return pl.pallas_call(
        functools.partial(_rwkv_kernel, tt=tt),
        grid=(b, t // tt),
        in_specs=[col(i) for i in range(5)] + [col(0), col(1)]
        + [pl.BlockSpec((None, tt // CHUNK, 1, MIX_HALF), lambda bi, ti: (bi, ti, 0, 0)), vec, vec],
        out_specs=pl.BlockSpec((None, tt, MIX_HALF), lambda bi, ti: (bi, ti, 0)),
        out_shape=jax.ShapeDtypeStruct((b, t, MIX_HALF), BF16),
        scratch_shapes=[pltpu.VMEM((PAIRS, LANES, LANES), F32)],
        compiler_params=_params(("arbitrary", "arbitrary")),
        name="rwkv7",
    )(*([ops] * 5), bg, bg, wc, lnw, lnb)


def _fox_attn_kernel(qa_ref, ka_ref, va_ref, og_ref, ow_ref, y_ref, m_ref, acc_ref, done_ref, *, tq):
    lane = _iota((1, LANES), 1)
    ones_h = _head_ones()

    def finish(tile):
        t0 = pl.multiple_of(tile * tq, tq)
        halves = []
        for h in range(2):
            acc = done_ref[h]
            denom = jnp.sum(jnp.where(lane == HEAD_DIM * (1 - h), acc, 0.0), axis=1, keepdims=True)
            halves.append(acc / denom)
        o = jnp.where(lane < HEAD_DIM, halves[0], halves[1])
        ms = _head_sum(o * o, ones_h) * (1.0 / HEAD_DIM)
        gate = _sigmoid(og_ref[pl.ds(t0, tq), :].astype(F32))
        y_ref[pl.ds(t0, tq), :] = (o * lax.rsqrt(ms + NORM_EPS) * ow_ref[...] * gate).astype(y_ref.dtype)

    def q_tile(i, carry):
        q0 = pl.multiple_of(i * tq, tq)
        q = [qa_ref[h, pl.ds(q0, tq), :] for h in range(2)]

        half = tq // 2
        parts = [(h, r0, nk) for r0, nk in ((0, half), (half, tq)) for h in range(2)]
        s = [_dot(q[h][r0:r0 + half, :], ka_ref[h, pl.ds(q0, nk), :], _NT) for h, r0, nk in parts]
        finish(jnp.maximum(i - 1, 0))
        s = [jnp.where(_iota((half, nk), 1) <= _iota((half, nk), 0) + r0, x, -jnp.inf)
             for x, (h, r0, nk) in zip(s, parts)]
        m_new = [jnp.broadcast_to(jnp.max(x, axis=1, keepdims=True), (half, LANES)) for x in s]
        p = [jnp.concatenate([jnp.exp2(x[:, c:c + LANES] - m) for c in range(0, x.shape[1], LANES)],
                             axis=1).astype(BF16) for x, m in zip(s, m_new)]
        pv = [_dot(x, va_ref[h, pl.ds(q0, nk), :]) for x, (h, r0, nk) in zip(p, parts)]
        for x, m, (h, r0, nk) in zip(pv, m_new, parts):
            acc_ref[h, r0:r0 + half, :] = x
            m_ref[h, r0:r0 + half, :] = m

        def step(start, nk):
            s = [_dot(q[h], ka_ref[h, pl.ds(start, nk), :], _NT) for h in range(2)]
            m_old = [m_ref[h] for h in range(2)]
            m_new = [jnp.maximum(m_old[h], jnp.max(s[h], axis=1, keepdims=True)) for h in range(2)]
            p = [jnp.concatenate([jnp.exp2(s[h][:, c:c + LANES] - m_new[h]) for c in range(0, nk, LANES)],
                                 axis=1).astype(BF16) for h in range(2)]
            pv = [_dot(p[h], va_ref[h, pl.ds(start, nk), :]) for h in range(2)]
            for h in range(2):
                acc_ref[h] = jnp.exp2(m_old[h] - m_new[h]) * acc_ref[h] + pv[h]
                m_ref[h] = m_new[h]

        width = FOX_KEY_TILES
        n_big = i // width

        def big_step(jj, c):
            step(pl.multiple_of(jj * (width * tq), width * tq), width * tq)
            return c

        lax.fori_loop(0, n_big, big_step, 0)
        rest = i - n_big * width
        done = n_big * width
        while width > 1:
            width //= 2

            @pl.when(rest % (2 * width) >= width)
            def _(width=width, done=done, rest=rest):
                tiles_before = done + (rest // (2 * width)) * (2 * width)
                step(pl.multiple_of(tiles_before * tq, width * tq), width * tq)

        done_ref[...] = acc_ref[...]
        return carry

    nq = qa_ref.shape[1] // tq
    done_ref[...] = jnp.ones_like(done_ref)
    lax.fori_loop(0, nq, q_tile, 0)
    finish(nq - 1)


def _fox_attn(qa, ka, va, og, ow, tq):
    b, _, t, _ = qa.shape
    seq_spec = pl.BlockSpec((None, 2, t, LANES), lambda bi, hp: (bi, hp, 0, 0))
    col_spec = pl.BlockSpec((None, t, LANES), lambda bi, hp: (bi, 0, hp))
    return pl.pallas_call(
        functools.partial(_fox_attn_kernel, tq=tq),
        grid=(b, PAIRS),
        in_specs=[seq_spec, seq_spec, seq_spec, col_spec, pl.BlockSpec((1, LANES), lambda bi, hp: (0, 0))],
        out_specs=col_spec,
        out_shape=jax.ShapeDtypeStruct((b, t, MIX_HALF), BF16),
        scratch_shapes=[pltpu.VMEM((2, tq, LANES), F32)] * 3,
        compiler_params=_params(("arbitrary", "arbitrary")),
        name="fox_attn",
    )(qa, ka, va, og, ow)


def _ffn_kernel(x_ref, yr_ref, yf_ref, wo_ref, nf_ref, wu_ref, cw_ref, cb_ref, wd_ref, nfin_ref, o_ref,
                x2_ref, hid_ref, carry_ref, *, tm):
    @pl.when(pl.program_id(1) == 0)
    def _():
        carry_ref[...] = jnp.zeros_like(carry_ref)

    x2 = (x_ref[...] + _dot(yr_ref[...], wo_ref[:MIX_HALF, :]) + _dot(yf_ref[...], wo_ref[MIX_HALF:, :]))
    x2_ref[...] = x2
    ms = jnp.mean(x2 * x2, axis=-1, keepdims=True)
    h2 = (x2 * lax.rsqrt(ms + NORM_EPS) * nf_ref[...]).astype(BF16)

    def up(lo, w):
        return [_dot(h2, wu_ref[:, off + lo:off + lo + w]) for off in (0, D_FF)]

    def conv(u, cols):
        ext = jnp.concatenate([carry_ref[:, cols], u], axis=0)
        carry_ref[:, cols] = u[tm - 8:, :]
        u1 = pltpu.roll(ext, 1, axis=0)[8:, :]
        u2 = pltpu.roll(ext, 2, axis=0)[8:, :]
        cw = cw_ref[:, cols]
        return cw[0:1, :] * u2 + cw[1:2, :] * u1 + cw[2:3, :] * u + cb_ref[:, cols]

    def conv_glu(lo, w, u):
        gate = conv(u[0], slice(lo, lo + w))
        val = conv(u[1], slice(D_FF + lo, D_FF + lo + w))
        hid_ref[:, lo:lo + w] = (gate * _sigmoid(gate) * val).astype(BF16)

    chunks = [(lo, min(FF_CHUNK, D_FF - lo)) for lo in range(0, D_FF, FF_CHUNK)]
    u_prev = up(*chunks[0])
    for prev, cur in zip(chunks[:-1], chunks[1:]):
        u_next = up(*cur)
        conv_glu(*prev, u_prev)
        u_prev = u_next
    conv_glu(*chunks[-1], u_prev)

    xo = x2_ref[...] + _dot(hid_ref[...], wd_ref[...])
    ms = jnp.mean(xo * xo, axis=-1, keepdims=True)
    o_ref[...] = xo * lax.rsqrt(ms + NORM_EPS) * nfin_ref[...]


def _ffn(x3, yr, yf, wo, nf, wu, cw, cb, wd, nfin, tm):
    b, t, _ = x3.shape
    row = lambda w: pl.BlockSpec((None, tm, w), lambda bi, ti: (bi, ti, 0))
    return pl.pallas_call(
        functools.partial(_ffn_kernel, tm=tm),
        grid=(b, t // tm),
        in_specs=[row(D_MODEL), row(MIX_HALF), row(MIX_HALF)]
        + [_resident(a.shape) for a in (wo, nf, wu, cw, cb, wd, nfin)],
        out_specs=row(D_MODEL),
        out_shape=jax.ShapeDtypeStruct(x3.shape, F32),
        scratch_shapes=[pltpu.VMEM((tm, D_MODEL), F32), pltpu.VMEM((tm, D_FF), BF16),
                        pltpu.VMEM((8, 2 * D_FF), F32)],
        compiler_params=_params(("arbitrary", "arbitrary")),
        name="outproj_convffn",
    )(x3, yr, yf, wo, nf, wu, cw, cb, wd, nfin)


def _pack_w_in(w_in):
    fox = w_in[:, RWKV_COLS:]
    fl = fox[:, 4 * MIX_HALF:]
    pad = jnp.zeros((D_MODEL, FOX_Q_COL - RWKV_COLS - fl.shape[1]), w_in.dtype)
    return jnp.concatenate([w_in[:, :RWKV_COLS], fl, pad, fox[:, :4 * MIX_HALF]], axis=1).astype(BF16)


def _tile(n, pref):
    t = min(n, pref)
    assert n % t == 0, (n, t)
    return t


def kernel(x, norm_mix_w, w_in, rwkv_mu, rwkv_w0, rwkv_w2, rwkv_a0, rwkv_a2, rwkv_g2, rwkv_k_k, rwkv_k_a,
           rwkv_r_k, rwkv_lnx_w, rwkv_lnx_b, fox_f_bias, fox_q_norm_w, fox_k_norm_w, fox_o_norm_w, w_out,
           norm_ffn_w, ffn_w_up, ffn_conv_w, ffn_conv_b, ffn_w_down, norm_final_w):
    b, t, d = x.shape
    assert d == D_MODEL and norm_mix_w.shape[0] == 1 and t % CHUNK == 0
    row = lambda a: a.reshape(1, -1).astype(F32)
    twice = lambda a: jnp.tile(a.reshape(1, HEAD_DIM), (1, 2)).astype(F32)

    fb = jnp.zeros((1, LANES), F32).at[0, :N_HEADS].set(fox_f_bias[0])
    zeros = jnp.zeros((DECAY_LORA, MIX_HALF), F32)
    w2p = jnp.concatenate([rwkv_w2[0], zeros], axis=0).astype(BF16)
    a2p = jnp.concatenate([zeros, rwkv_a2[0]], axis=0).astype(BF16)
    rwkv_consts = [row(a[0]) for a in (rwkv_mu, rwkv_w0, rwkv_a0, rwkv_k_k, rwkv_k_a, rwkv_r_k)]
    rwkv_consts += [w2p, a2p, rwkv_g2[0].astype(BF16)]
    ops, bg, wc, og, qa, ka, va = _inproj(x, row(norm_mix_w[0]), _pack_w_in(w_in[0]), fb, twice(fox_q_norm_w[0]),
                                          twice(fox_k_norm_w[0]), rwkv_consts, _tile(t, TILE_INPROJ))

    y_rwkv = _rwkv(ops, bg, wc, row(rwkv_lnx_w[0]), row(rwkv_lnx_b[0]), _tile(t, TILE_RWKV))

    y_fox = _fox_attn(qa, ka, va, og, twice(fox_o_norm_w[0]), _tile(t, TILE_FOX_ATTN))

    out = _ffn(x, y_rwkv, y_fox, w_out[0].astype(BF16), row(norm_ffn_w[0]),
               ffn_w_up[0].astype(BF16), ffn_conv_w[0].astype(F32), row(ffn_conv_b[0]),
               ffn_w_down[0].astype(BF16), row(norm_final_w),
               _tile(t, TILE_FFN))
    return out.astype(x.dtype)
```

```python
import functools

import jax
import jax.numpy as jnp
import numpy as np
from jax import lax
from jax.experimental import pallas as pl
from jax.experimental.pallas import tpu as pltpu

F32 = jnp.float32
BF16 = jnp.bfloat16

D_MODEL = 1024
HEAD_DIM = 64
N_HEADS = 8
MIX_HALF = N_HEADS * HEAD_DIM
DECAY_LORA = 64
AAA_LORA = 64
GATE_LORA = 128
RWKV_COLS = 3 * MIX_HALF + DECAY_LORA + AAA_LORA + GATE_LORA
D_FF = 2816
NORM_EPS = 1e-6
LNX_EPS = 64e-5

LANES = 128
PAIRS = MIX_HALF // LANES
CHUNK = 64
FL_BLOCK = RWKV_COLS // LANES
FOX_Q_COL = 2048
FF_CHUNK = 512
CUMSUM_ROWS = 256
VMEM_LIMIT = 56 * 1024 * 1024
LOG2E = 1.4426950408889634

TILE_INPROJ = 512
TILE_RWKV = 512
TILE_FOX_ATTN = 1024
FOX_KEY_TILES = 2
TILE_FFN = 512


def _params(sem):
    return pltpu.CompilerParams(dimension_semantics=sem, vmem_limit_bytes=VMEM_LIMIT)


def _split3(x):
    hi = x.astype(BF16)
    r1 = x - hi.astype(F32)
    mid = r1.astype(BF16)
    lo = (r1 - mid.astype(F32)).astype(BF16)
    return hi, mid, lo


def _dot(a, b, dims=None):
    if dims is None:
        dims = (((a.ndim - 1,), (0,)), ((), ()))
    return lax.dot_general(a, b, dims, preferred_element_type=F32)


_NT = (((1,), (1,)), ((), ()))

_BNN = (((2,), (1,)), ((0,), (0,)))
_BNT = (((2,), (2,)), ((0,), (0,)))
_BTN = (((1,), (1,)), ((0,), (0,)))


def _bmm(a, b, dims=_BNN):
    return lax.dot_general(a.astype(BF16), b.astype(BF16), dims, preferred_element_type=F32)


def _dot_exact_lhs(lhs_bf16, x):
    hi, mid, lo = _split3(x)
    return _dot(lhs_bf16, hi) + _dot(lhs_bf16, mid) + _dot(lhs_bf16, lo)


def _iota(shape, dim):
    return lax.broadcasted_iota(jnp.int32, shape, dim)


def _head_ones():
    r = _iota((LANES, LANES), 0) // HEAD_DIM
    c = _iota((LANES, LANES), 1) // HEAD_DIM
    return jnp.where(r == c, 1.0, 0.0).astype(BF16)


def _head_sum(x, ones_h):
    xb = x.astype(BF16)
    return jnp.concatenate([_dot(xb[:, i:i + LANES], ones_h) for i in range(0, x.shape[1], LANES)], axis=1)


def _block_tri(n):
    r = _iota((n, n), 0)
    c = _iota((n, n), 1)
    return jnp.where((r // CHUNK == c // CHUNK) & (c <= r), 1.0, 0.0).astype(BF16)


def _sigmoid(z):
    return 1.0 / (1.0 + jnp.exp(-z))


def _softplus(z):
    return jnp.maximum(z, 0.0) + jnp.log1p(jnp.exp(-jnp.abs(z)))


def _resident(shape):
    return pl.BlockSpec(shape, lambda *_: (0,) * len(shape), pipeline_mode=pl.Buffered(1))


def _aug_tables():
    sel = np.zeros((LANES, N_HEADS * LANES), np.float32)
    for h in range(N_HEADS):
        a0 = h * LANES + HEAD_DIM * (1 - h % 2)
        for i in range(3):
            sel[i * N_HEADS + h, a0 + i] = 1.0
            sel[i * N_HEADS + h, a0 + 3 + i] = -1.0
    return jnp.asarray(sel, BF16)


def _inproj_kernel(x_ref, nw_ref, w_ref, fb_ref, qw_ref, kw_ref, sel_ref,
                   mu_ref, w0_ref, a0_ref, kkw_ref, kaw_ref, rkw_ref, w2_ref, a2_ref, g2_ref,
                   ops_ref, bg_ref, wc_ref, og_ref, qa_ref, ka_ref, va_ref, carry_ref, prev_ref, *, tm):
    @pl.when(pl.program_id(1) == 0)
    def _():
        carry_ref[...] = jnp.zeros_like(carry_ref)
        prev_ref[...] = jnp.zeros_like(prev_ref)

    x = x_ref[...]
    ms = jnp.mean(x * x, axis=-1, keepdims=True)
    h = (x * lax.rsqrt(ms + NORM_EPS) * nw_ref[...]).astype(BF16)
    half_w = MIX_HALF // 2
    proj = {}

    def project(name, lo, width=half_w):
        proj[name] = _dot(h, w_ref[:, lo:lo + width])

    def matmuls():
        project("fl", FL_BLOCK * LANES, LANES)
        for j, name in enumerate(("fq0", "fk0", "fv0")):
            project(name, FOX_Q_COL + j * MIX_HALF)
        yield
        for j, name in enumerate(("fq1", "fk1", "fv1")):
            project(name, FOX_Q_COL + j * MIX_HALF + half_w)
            yield
        project("wg", 3 * MIX_HALF)
        yield
        for j, name in ((1, "rk0"), (1, "rk1"), (0, "rr0"), (0, "rr1"), (2, "rv0"), (2, "rv1")):
            project(name, j * MIX_HALF + int(name[-1]) * half_w)
            yield
        og = FOX_Q_COL + 3 * MIX_HALF
        og_ref[:, :half_w] = _dot(h, w_ref[:, og:og + half_w]).astype(BF16)
        yield
        og_ref[:, half_w:] = _dot(h, w_ref[:, og + half_w:og + MIX_HALF]).astype(BF16)

    lane = _iota((1, LANES), 1)
    ones_h = _head_ones()

    def shifted(name, lo):
        p = proj.pop(name)
        cols = slice(lo, lo + p.shape[1])
        prev = jnp.where(_iota(p.shape, 0) == 0, prev_ref[7:8, cols], pltpu.roll(p, 1, axis=0))
        prev_ref[:, cols] = p[tm - 8:, :]
        return p + (prev - p) * mu_ref[:, cols]

    def prepare():
        logf = -_softplus(-(proj.pop("fl") + fb_ref[...]))
        blk = min(tm, CUMSUM_ROWS)
        tri = jnp.where(_iota((blk, blk), 1) <= _iota((blk, blk), 0), 1.0, 0.0).astype(BF16)
        run = carry_ref[0:1, :]
        c_blocks = []
        for r0 in range(0, tm, blk):
            cb = _dot_exact_lhs(tri, logf[r0:r0 + blk, :]) + run
            run = cb[blk - 1:blk, :]
            c_blocks.append(cb)
        carry_ref[...] = jnp.broadcast_to(run, carry_ref.shape)
        c = jnp.concatenate(c_blocks, axis=0)
        c_hi, c_mid, c_lo = (p.astype(F32) for p in _split3(c * LOG2E))
        packed = jnp.where(lane < N_HEADS, c_hi, jnp.where(lane < 2 * N_HEADS, pltpu.roll(c_mid, N_HEADS, axis=1),
                                                           pltpu.roll(c_lo, 2 * N_HEADS, axis=1)))
        aug = _dot(packed.astype(BF16), sel_ref[...])
        r2 = _iota((2 * LANES, 2 * LANES), 0) // HEAD_DIM
        c2 = _iota((2 * LANES, 2 * LANES), 1) // HEAD_DIM
        ones_qk = jnp.where(r2 == c2, 1.0, 0.0).astype(BF16)
        yield
        for pb in range(PAIRS):
            sl = slice((pb % 2) * LANES, (pb % 2 + 1) * LANES)
            q, k, v = (proj[name + str(pb // 2)][:, sl] for name in ("fq", "fk", "fv"))
            ss = _dot(jnp.concatenate([q * q, k * k], axis=1).astype(BF16), ones_qk) * (1.0 / HEAD_DIM)
            qn = q * lax.rsqrt(ss[:, :LANES] + NORM_EPS) * (qw_ref[...] * (HEAD_DIM ** -0.5 * LOG2E))
            kn = k * lax.rsqrt(ss[:, LANES:] + NORM_EPS) * kw_ref[...]
            for half in range(2):
                hd = 2 * pb + half
                own = (lane // HEAD_DIM) == half
                a0 = HEAD_DIM * (1 - half)
                z = aug[:, hd * LANES:(hd + 1) * LANES]
                c_q = (lane >= a0) & (lane < a0 + 3)
                c_k = (lane >= a0 + 3) & (lane < a0 + 6)
                qa = jnp.where(own, qn, jnp.where(c_q, z, jnp.where(c_k, 1.0, 0.0)))
                ka = jnp.where(own, kn, jnp.where(c_k, z, jnp.where(c_q, 1.0, 0.0)))
                va = jnp.where(own, v, jnp.where(lane == a0, 1.0, 0.0))
                qa_ref[hd] = qa.astype(BF16)
                ka_ref[hd] = ka.astype(BF16)
                va_ref[hd] = va.astype(BF16)
            yield

        wg = shifted("wg", 3 * MIX_HALF)
        tanh_wa = jnp.tanh(wg[:, :LANES]).astype(BF16)
        wa = wg[:, :LANES].astype(BF16)
        sig_gl = _sigmoid(wg[:, LANES:]).astype(BF16)
        tri = _block_tri(blk)
        st = []
        for j in range(2):
            cj = slice(j * half_w, (j + 1) * half_w)
            w = -_softplus(-(w0_ref[:, cj] + _dot(tanh_wa, w2_ref[:, cj]))) - 0.5
            logd = -jnp.exp(w)
            a_sig = _sigmoid(a0_ref[:, cj] + _dot(wa, a2_ref[:, cj]))
            bg_ref[:, MIX_HALF + j * half_w:MIX_HALF + (j + 1) * half_w] = _dot(sig_gl, g2_ref[:, cj]).astype(BF16)
            cs = jnp.concatenate([_dot_exact_lhs(tri, logd[r0:r0 + blk, :]) for r0 in range(0, tm, blk)], axis=0)
            w_incl = jnp.exp(cs)
            wc_ref[:, :, cj] = w_incl.reshape(tm // CHUNK, CHUNK, half_w)[:, CHUNK - 1:, :]
            st.append(dict(a_sig=a_sig, w_incl=w_incl, w_inv=jnp.exp(-cs), w_excl=jnp.exp(cs - logd)))
            yield
        group = lambda g, j: slice(g * MIX_HALF + j * half_w, g * MIX_HALF + (j + 1) * half_w)
        for j in range(2):
            k = shifted("rk%d" % j, MIX_HALF + j * half_w)
            kk = k * kkw_ref[:, group(0, j)]
            kk = kk * lax.rsqrt(jnp.maximum(_head_sum(kk * kk, ones_h), 1e-24))
            st[j]["k2"] = k * (1.0 + (st[j]["a_sig"] - 1.0) * kaw_ref[:, group(0, j)])
            ops_ref[:, group(0, j)] = ((-kk) * st[j]["w_excl"]).astype(BF16)
            ops_ref[:, group(2, j)] = (kk * st[j]["a_sig"] * st[j]["w_inv"]).astype(BF16)
            ops_ref[:, group(3, j)] = (st[j]["k2"] * st[j]["w_inv"]).astype(BF16)
            yield
        for j in range(2):
            st[j]["r"] = shifted("rr%d" % j, j * half_w)
            ops_ref[:, group(1, j)] = (st[j]["r"] * st[j]["w_incl"]).astype(BF16)
            yield
        for j in range(2):
            v = shifted("rv%d" % j, 2 * MIX_HALF + j * half_w)
            ops_ref[:, group(4, j)] = v.astype(BF16)
            bonus = _head_sum(st[j]["r"] * st[j]["k2"] * rkw_ref[:, group(0, j)], ones_h) * v
            bg_ref[:, group(0, j)] = bonus.astype(BF16)
            yield

    streams = [matmuls(), prepare()]
    while streams:
        for g in list(streams):
            if next(g, streams) is streams:
                streams.remove(g)


def _inproj(x3, norm_w, w_packed, f_bias, qw, kw, rwkv_consts, tm):
    b, t, _ = x3.shape
    row = lambda w: pl.BlockSpec((None, tm, w), lambda bi, ti: (bi, ti, 0))
    head_spec = pl.BlockSpec((None, N_HEADS, tm, LANES), lambda bi, ti: (bi, 0, ti, 0))
    head_shape = jax.ShapeDtypeStruct((b, N_HEADS, t, LANES), BF16)
    consts = (norm_w, w_packed, f_bias, qw, kw, _aug_tables()) + tuple(rwkv_consts)
    return pl.pallas_call(
        functools.partial(_inproj_kernel, tm=tm),
        grid=(b, t // tm),
        in_specs=[row(D_MODEL)] + [_resident(a.shape) for a in consts],
        out_specs=[row(5 * MIX_HALF), row(2 * MIX_HALF),
                   pl.BlockSpec((None, tm // CHUNK, 1, MIX_HALF), lambda bi, ti: (bi, ti, 0, 0)),
                   row(MIX_HALF), head_spec, head_spec, head_spec],
        out_shape=[jax.ShapeDtypeStruct((b, t, 5 * MIX_HALF), BF16), jax.ShapeDtypeStruct((b, t, 2 * MIX_HALF), BF16),
                   jax.ShapeDtypeStruct((b, t // CHUNK, 1, MIX_HALF), F32),
                   jax.ShapeDtypeStruct((b, t, MIX_HALF), BF16), head_shape, head_shape, head_shape],
        scratch_shapes=[pltpu.VMEM((8, LANES), F32), pltpu.VMEM((8, RWKV_COLS), F32)],
        compiler_params=_params(("arbitrary", "arbitrary")),
        name="inproj_foxprep",
    )(x3, *consts)


def _rwkv_kernel(a_ref, r_ref, b_ref, k_ref, v_ref, bonus_ref, gate_ref, wc_ref, lnw_ref, lnb_ref,
                 y_ref, s_ref, *, tt):
    @pl.when(pl.program_id(1) == 0)
    def _():
        s_ref[...] = jnp.zeros_like(s_ref)

    nc = tt // CHUNK
    ones_h = _head_ones()
    m0 = _iota((1, 1, LANES), 2) < HEAD_DIM

    def units(x_ref):
        out = []
        for p in range(PAIRS):
            xp = x_ref[:, p * LANES:(p + 1) * LANES].reshape(nc, CHUNK, LANES)
            zero = jnp.zeros_like(xp)
            out.append(jnp.concatenate([jnp.where(m0, xp, zero), jnp.where(m0, zero, xp)], axis=1))
        return jnp.concatenate(out, axis=0)

    a_st, r_st, b_st, k_st, v_st = (units(ref) for ref in (a_ref, r_ref, b_ref, k_ref, v_ref))
    wc = jnp.concatenate([wc_ref[:, :, p * LANES:(p + 1) * LANES] for p in range(PAIRS)], axis=0)

    ri = _iota((1, LANES, LANES), 1)
    ci = _iota((1, LANES, LANES), 2)
    same = (ri // CHUNK) == (ci // CHUNK)
    strict = same & (ci < ri)
    incl = same & (ci <= ri)
    eye = jnp.where(ri == ci, 1.0, 0.0).astype(F32)

    bk = jnp.concatenate([b_st, k_st], axis=1)
    sc = _bmm(jnp.concatenate([a_st, r_st], axis=1), bk, _BNT)
    l_ab = jnp.where(strict, sc[:, :LANES, :LANES], 0.0)
    l_ak = jnp.where(strict, sc[:, :LANES, LANES:], 0.0)
    m_rbk = jnp.concatenate([jnp.where(incl, sc[:, LANES:, :LANES], 0.0),
                             jnp.where(incl, sc[:, LANES:, LANES:], 0.0)], axis=2).astype(BF16)

    t_inv = eye + l_ab
    lp = l_ab.astype(BF16)
    lp = _bmm(lp, lp).astype(BF16)
    for _ in range(CHUNK.bit_length() - 3):
        both = _bmm(jnp.concatenate([lp, t_inv.astype(BF16)], axis=1), lp)
        t_inv = t_inv + both[:, LANES:]
        lp = both[:, :LANES].astype(BF16)
    t_inv = t_inv + _bmm(t_inv, lp)

    lakv = _bmm(l_ak, v_st)
    au = _bmm(t_inv, jnp.concatenate([a_st, lakv.astype(BF16)], axis=2))
    au_b = au.astype(BF16)
    zv = jnp.concatenate([jnp.zeros_like(v_st), v_st], axis=2)
    ry = _bmm(m_rbk, jnp.concatenate([au_b, zv], axis=1))
    r2_st = r_st.astype(F32) + ry[:, :, :LANES]
    y0_st = ry[:, :, LANES:]
    g_mat = (eye + _bmm(au_b[:, :, :LANES], b_st, _BTN)) * wc
    h_mat = _bmm(jnp.concatenate([au_b[:, :, LANES:], v_st], axis=1), bk, _BTN) * wc

    def chunk(x, c):
        return x.reshape((PAIRS, nc) + x.shape[1:])[:, c]

    s = s_ref[...]
    ys = []
    for c in range(nc):
        y_st = _bmm(chunk(r2_st, c), s, _BNT) + chunk(y0_st, c)
        s = _bmm(s, chunk(g_mat, c)) + chunk(h_mat, c)
        ys.append(y_st[:, :CHUNK] + y_st[:, CHUNK:])
    s_ref[...] = s
    y = jnp.concatenate([jnp.concatenate([yc[p] for yc in ys], axis=0) for p in range(PAIRS)], axis=1)

    mean = _head_sum(y, ones_h) * (1.0 / HEAD_DIM)
    yc = y - mean
    var = _head_sum(yc * yc, ones_h) * (1.0 / HEAD_DIM)
    yn = yc * lax.rsqrt(var + LNX_EPS) * lnw_ref[...] + lnb_ref[...]
    y_ref[...] = ((yn + bonus_ref[...].astype(F32)) * gate_ref[...].astype(F32)).astype(y_ref.dtype)


def _rwkv(ops, bg, wc, lnw, lnb, tt):
    b, t, _ = ops.shape
    col = lambda cb: pl.BlockSpec((None, tt, MIX_HALF), lambda bi, ti, cb=cb: (bi, ti, cb))
    vec = pl.BlockSpec((1, MIX_HALF), lambda bi, ti: (0, 0))
    return pl.pallas_call(
        functools.partial(_rwkv_kernel, tt=tt),
        grid=(b, t // tt),
        in_specs=[col(i) for i in range(5)] + [col(0), col(1)]
        + [pl.BlockSpec((None, tt // CHUNK, 1, MIX_HALF), lambda bi, ti: (bi, ti, 0, 0)), vec, vec],
        out_specs=pl.BlockSpec((None, tt, MIX_HALF), lambda bi, ti: (bi, ti, 0)),
        out_shape=jax.ShapeDtypeStruct((b, t, MIX_HALF), BF16),
        scratch_shapes=[pltpu.VMEM((PAIRS, LANES, LANES), F32)],
        compiler_params=_params(("arbitrary", "arbitrary")),
        name="rwkv7",
    )(*([ops] * 5), bg, bg, wc, lnw, lnb)


def _fox_attn_kernel(qa_ref, ka_ref, va_ref, og_ref, ow_ref, y_ref, m_ref, acc_ref, done_ref, *, tq):
    lane = _iota((1, LANES), 1)
    ones_h = _head_ones()

    def finish(tile):
        t0 = pl.multiple_of(tile * tq, tq)
        halves = []
        for h in range(2):
            acc = done_ref[h]
            denom = jnp.sum(jnp.where(lane == HEAD_DIM * (1 - h), acc, 0.0), axis=1, keepdims=True)
            halves.append(acc / denom)
        o = jnp.where(lane < HEAD_DIM, halves[0], halves[1])
        ms = _head_sum(o * o, ones_h) * (1.0 / HEAD_DIM)
        gate = _sigmoid(og_ref[pl.ds(t0, tq), :].astype(F32))
        y_ref[pl.ds(t0, tq), :] = (o * lax.rsqrt(ms + NORM_EPS) * ow_ref[...] * gate).astype(y_ref.dtype)

    def q_tile(i, carry):
        q0 = pl.multiple_of(i * tq, tq)
        q = [qa_ref[h, pl.ds(q0, tq), :] for h in range(2)]

        half = tq // 2
        parts = [(h, r0, nk) for r0, nk in ((0, half), (half, tq)) for h in range(2)]
        s = [_dot(q[h][r0:r0 + half, :], ka_ref[h, pl.ds(q0, nk), :], _NT) for h, r0, nk in parts]
        finish(jnp.maximum(i - 1, 0))
        s = [jnp.where(_iota((half, nk), 1) <= _iota((half, nk), 0) + r0, x, -jnp.inf)
             for x, (h, r0, nk) in zip(s, parts)]
        m_new = [jnp.broadcast_to(jnp.max(x, axis=1, keepdims=True), (half, LANES)) for x in s]
        p = [jnp.concatenate([jnp.exp2(x[:, c:c + LANES] - m) for c in range(0, x.shape[1], LANES)],
                             axis=1).astype(BF16) for x, m in zip(s, m_new)]
        pv = [_dot(x, va_ref[h, pl.ds(q0, nk), :]) for x, (h, r0, nk) in zip(p, parts)]
        for x, m, (h, r0, nk) in zip(pv, m_new, parts):
            acc_ref[h, r0:r0 + half, :] = x
            m_ref[h, r0:r0 + half, :] = m

        def step(start, nk):
            s = [_dot(q[h], ka_ref[h, pl.ds(start, nk), :], _NT) for h in range(2)]
            m_old = [m_ref[h] for h in range(2)]
            m_new = [jnp.maximum(m_old[h], jnp.max(s[h], axis=1, keepdims=True)) for h in range(2)]
            p = [jnp.concatenate([jnp.exp2(s[h][:, c:c + LANES] - m_new[h]) for c in range(0, nk, LANES)],
                                 axis=1).astype(BF16) for h in range(2)]
            pv = [_dot(p[h], va_ref[h, pl.ds(start, nk), :]) for h in range(2)]
            for h in range(2):
                acc_ref[h] = jnp.exp2(m_old[h] - m_new[h]) * acc_ref[h] + pv[h]
                m_ref[h] = m_new[h]

        width = FOX_KEY_TILES
        n_big = i // width

        def big_step(jj, c):
            step(pl.multiple_of(jj * (width * tq), width * tq), width * tq)
            return c

        lax.fori_loop(0, n_big, big_step, 0)
        rest = i - n_big * width
        done = n_big * width
        while width > 1:
            width //= 2

            @pl.when(rest % (2 * width) >= width)
            def _(width=width, done=done, rest=rest):
                tiles_before = done + (rest // (2 * width)) * (2 * width)
                step(pl.multiple_of(tiles_before * tq, width * tq), width * tq)

        done_ref[...] = acc_ref[...]
        return carry

    nq = qa_ref.shape[1] // tq
    done_ref[...] = jnp.ones_like(done_ref)
    lax.fori_loop(0, nq, q_tile, 0)
    finish(nq - 1)


def _fox_attn(qa, ka, va, og, ow, tq):
    b, _, t, _ = qa.shape
    seq_spec = pl.BlockSpec((None, 2, t, LANES), lambda bi, hp: (bi, hp, 0, 0))
    col_spec = pl.BlockSpec((None, t, LANES), lambda bi, hp: (bi, 0, hp))
    return pl.pallas_call(
        functools.partial(_fox_attn_kernel, tq=tq),
        grid=(b, PAIRS),
        in_specs=[seq_spec, seq_spec, seq_spec, col_spec, pl.BlockSpec((1, LANES), lambda bi, hp: (0, 0))],
        out_specs=col_spec,
        out_shape=jax.ShapeDtypeStruct((b, t, MIX_HALF), BF16),
        scratch_shapes=[pltpu.VMEM((2, tq, LANES), F32)] * 3,
        compiler_params=_params(("arbitrary", "arbitrary")),
        name="fox_attn",
    )(qa, ka, va, og, ow)


def _ffn_kernel(x_ref, yr_ref, yf_ref, wo_ref, nf_ref, wu_ref, cw_ref, cb_ref, wd_ref, nfin_ref, o_ref,
                x2_ref, hid_ref, carry_ref, *, tm):
    @pl.when(pl.program_id(1) == 0)
    def _():
        carry_ref[...] = jnp.zeros_like(carry_ref)

    x2 = (x_ref[...] + _dot(yr_ref[...], wo_ref[:MIX_HALF, :]) + _dot(yf_ref[...], wo_ref[MIX_HALF:, :]))
    x2_ref[...] = x2
    ms = jnp.mean(x2 * x2, axis=-1, keepdims=True)
    h2 = (x2 * lax.rsqrt(ms + NORM_EPS) * nf_ref[...]).astype(BF16)

    def up(lo, w):
        return [_dot(h2, wu_ref[:, off + lo:off + lo + w]) for off in (0, D_FF)]

    def conv(u, cols):
        ext = jnp.concatenate([carry_ref[:, cols], u], axis=0)
        carry_ref[:, cols] = u[tm - 8:, :]
        u1 = pltpu.roll(ext, 1, axis=0)[8:, :]
        u2 = pltpu.roll(ext, 2, axis=0)[8:, :]
        cw = cw_ref[:, cols]
        return cw[0:1, :] * u2 + cw[1:2, :] * u1 + cw[2:3, :] * u + cb_ref[:, cols]

    def conv_glu(lo, w, u):
        gate = conv(u[0], slice(lo, lo + w))
        val = conv(u[1], slice(D_FF + lo, D_FF + lo + w))
        hid_ref[:, lo:lo + w] = (gate * _sigmoid(gate) * val).astype(BF16)

    chunks = [(lo, min(FF_CHUNK, D_FF - lo)) for lo in range(0, D_FF, FF_CHUNK)]
    u_prev = up(*chunks[0])
    for prev, cur in zip(chunks[:-1], chunks[1:]):
        u_next = up(*cur)
        conv_glu(*prev, u_prev)
        u_prev = u_next
    conv_glu(*chunks[-1], u_prev)

    xo = x2_ref[...] + _dot(hid_ref[...], wd_ref[...])
    ms = jnp.mean(xo * xo, axis=-1, keepdims=True)
    o_ref[...] = xo * lax.rsqrt(ms + NORM_EPS) * nfin_ref[...]


def _ffn(x3, yr, yf, wo, nf, wu, cw, cb, wd, nfin, tm):
    b, t, _ = x3.shape
    row = lambda w: pl.BlockSpec((None, tm, w), lambda bi, ti: (bi, ti, 0))
    return pl.pallas_call(
        functools.partial(_ffn_kernel, tm=tm),
        grid=(b, t // tm),
        in_specs=[row(D_MODEL), row(MIX_HALF), row(MIX_HALF)]
        + [_resident(a.shape) for a in (wo, nf, wu, cw, cb, wd, nfin)],
        out_specs=row(D_MODEL),
        out_shape=jax.ShapeDtypeStruct(x3.shape, F32),
        scratch_shapes=[pltpu.VMEM((tm, D_MODEL), F32), pltpu.VMEM((tm, D_FF), BF16),
                        pltpu.VMEM((8, 2 * D_FF), F32)],
        compiler_params=_params(("arbitrary", "arbitrary")),
        name="outproj_convffn",
    )(x3, yr, yf, wo, nf, wu, cw, cb, wd, nfin)


def _pack_w_in(w_in):
    fox = w_in[:, RWKV_COLS:]
    fl = fox[:, 4 * MIX_HALF:]
    pad = jnp.zeros((D_MODEL, FOX_Q_COL - RWKV_COLS - fl.shape[1]), w_in.dtype)
    return jnp.concatenate([w_in[:, :RWKV_COLS], fl, pad, fox[:, :4 * MIX_HALF]], axis=1).astype(BF16)


def _tile(n, pref):
    t = min(n, pref)
    assert n % t == 0, (n, t)
    return t


def kernel(x, norm_mix_w, w_in, rwkv_mu, rwkv_w0, rwkv_w2, rwkv_a0, rwkv_a2, rwkv_g2, rwkv_k_k, rwkv_k_a,
           rwkv_r_k, rwkv_lnx_w, rwkv_lnx_b, fox_f_bias, fox_q_norm_w, fox_k_norm_w, fox_o_norm_w, w_out,
           norm_ffn_w, ffn_w_up, ffn_conv_w, ffn_conv_b, ffn_w_down, norm_final_w):
    b, t, d = x.shape
    assert d == D_MODEL and norm_mix_w.shape[0] == 1 and t % CHUNK == 0
    row = lambda a: a.reshape(1, -1).astype(F32)
    twice = lambda a: jnp.tile(a.reshape(1, HEAD_DIM), (1, 2)).astype(F32)

    fb = jnp.zeros((1, LANES), F32).at[0, :N_HEADS].set(fox_f_bias[0])
    zeros = jnp.zeros((DECAY_LORA, MIX_HALF), F32)
    w2p = jnp.concatenate([rwkv_w2[0], zeros], axis=0).astype(BF16)
    a2p = jnp.concatenate([zeros, rwkv_a2[0]], axis=0).astype(BF16)
    rwkv_consts = [row(a[0]) for a in (rwkv_mu, rwkv_w0, rwkv_a0, rwkv_k_k, rwkv_k_a, rwkv_r_k)]
    rwkv_consts += [w2p, a2p, rwkv_g2[0].astype(BF16)]
    ops, bg, wc, og, qa, ka, va = _inproj(x, row(norm_mix_w[0]), _pack_w_in(w_in[0]), fb, twice(fox_q_norm_w[0]),
                                          twice(fox_k_norm_w[0]), rwkv_consts, _tile(t, TILE_INPROJ))

    y_rwkv = _rwkv(ops, bg, wc, row(rwkv_lnx_w[0]), row(rwkv_lnx_b[0]), _tile(t, TILE_RWKV))

    y_fox = _fox_attn(qa, ka, va, og, twice(fox_o_norm_w[0]), _tile(t, TILE_FOX_ATTN))

    out = _ffn(x, y_rwkv, y_fox, w_out[0].astype(BF16), row(norm_ffn_w[0]),
               ffn_w_up[0].astype(BF16), ffn_conv_w[0].astype(F32), row(ffn_conv_b[0]),
               ffn_w_down[0].astype(BF16), row(norm_final_w),
               _tile(t, TILE_FFN))
    return out.astype(x.dtype)
```

```python
import functools

import jax
import jax.numpy as jnp
import numpy as np
from jax import lax
from jax.experimental import pallas as pl
from jax.experimental.pallas import tpu as pltpu

F32 = jnp.float32
BF16 = jnp.bfloat16

D_MODEL = 1024
HEAD_DIM = 64
N_HEADS = 8
MIX_HALF = N_HEADS * HEAD_DIM
DECAY_LORA = 64
AAA_LORA = 64
GATE_LORA = 128
RWKV_COLS = 3 * MIX_HALF + DECAY_LORA + AAA_LORA + GATE_LORA
D_FF = 2816
NORM_EPS = 1e-6
LNX_EPS = 64e-5

LANES = 128
PAIRS = MIX_HALF // LANES
CHUNK = 64
FL_BLOCK = RWKV_COLS // LANES
FOX_Q_COL = 2048
FF_CHUNK = 512
CUMSUM_ROWS = 256
VMEM_LIMIT = 56 * 1024 * 1024
LOG2E = 1.4426950408889634

TILE_INPROJ = 512
TILE_RWKV = 512
TILE_FOX_ATTN = 1024
FOX_KEY_TILES = 2
FOX_DIAG_SPLITS = 4
TILE_FFN = 512


def _params(sem):
    return pltpu.CompilerParams(dimension_semantics=sem, vmem_limit_bytes=VMEM_LIMIT)


def _split3(x):
    hi = x.astype(BF16)
    r1 = x - hi.astype(F32)
    mid = r1.astype(BF16)
    lo = (r1 - mid.astype(F32)).astype(BF16)
    return hi, mid, lo


def _dot(a, b, dims=None):
    if dims is None:
        dims = (((a.ndim - 1,), (0,)), ((), ()))
    return lax.dot_general(a, b, dims, preferred_element_type=F32)


_NT = (((1,), (1,)), ((), ()))

_BNN = (((2,), (1,)), ((0,), (0,)))
_BNT = (((2,), (2,)), ((0,), (0,)))
_BTN = (((1,), (1,)), ((0,), (0,)))


def _bmm(a, b, dims=_BNN):
    return lax.dot_general(a.astype(BF16), b.astype(BF16), dims, preferred_element_type=F32)


def _dot_exact_lhs(lhs_bf16, x):
    hi, mid, lo = _split3(x)
    return _dot(lhs_bf16, hi) + _dot(lhs_bf16, mid) + _dot(lhs_bf16, lo)


def _iota(shape, dim):
    return lax.broadcasted_iota(jnp.int32, shape, dim)


def _head_ones():
    r = _iota((LANES, LANES), 0) // HEAD_DIM
    c = _iota((LANES, LANES), 1) // HEAD_DIM
    return jnp.where(r == c, 1.0, 0.0).astype(BF16)


def _head_sum(x, ones_h):
    xb = x.astype(BF16)
    return jnp.concatenate([_dot(xb[:, i:i + LANES], ones_h) for i in range(0, x.shape[1], LANES)], axis=1)


def _block_tri(n):
    r = _iota((n, n), 0)
    c = _iota((n, n), 1)
    return jnp.where((r // CHUNK == c // CHUNK) & (c <= r), 1.0, 0.0).astype(BF16)


def _sigmoid(z):
    return 1.0 / (1.0 + jnp.exp(-z))


def _softplus(z):
    return jnp.maximum(z, 0.0) + jnp.log1p(jnp.exp(-jnp.abs(z)))


def _resident(shape):
    return pl.BlockSpec(shape, lambda *_: (0,) * len(shape), pipeline_mode=pl.Buffered(1))


def _aug_tables():
    sel = np.zeros((LANES, N_HEADS * LANES), np.float32)
    for h in range(N_HEADS):
        a0 = h * LANES + HEAD_DIM * (1 - h % 2)
        for i in range(3):
            sel[i * N_HEADS + h, a0 + i] = 1.0
            sel[i * N_HEADS + h, a0 + 3 + i] = -1.0
    return jnp.asarray(sel, BF16)


def _inproj_kernel(x_ref, nw_ref, w_ref, fb_ref, qw_ref, kw_ref, sel_ref,
                   mu_ref, w0_ref, a0_ref, kkw_ref, kaw_ref, rkw_ref, w2_ref, a2_ref, g2_ref,
                   ops_ref, bg_ref, wc_ref, og_ref, qa_ref, ka_ref, va_ref, carry_ref, prev_ref, *, tm):
    @pl.when(pl.program_id(1) == 0)
    def _():
        carry_ref[...] = jnp.zeros_like(carry_ref)
        prev_ref[...] = jnp.zeros_like(prev_ref)

    x = x_ref[...]
    ms = jnp.mean(x * x, axis=-1, keepdims=True)
    h = (x * lax.rsqrt(ms + NORM_EPS) * nw_ref[...]).astype(BF16)
    half_w = MIX_HALF // 2
    proj = {}

    def project(name, lo, width=half_w):
        proj[name] = _dot(h, w_ref[:, lo:lo + width])

    def matmuls():
        project("fl", FL_BLOCK * LANES, LANES)
        for j, name in enumerate(("fq0", "fk0", "fv0")):
            project(name, FOX_Q_COL + j * MIX_HALF)
        yield
        for j, name in enumerate(("fq1", "fk1", "fv1")):
            project(name, FOX_Q_COL + j * MIX_HALF + half_w)
            yield
        project("wg", 3 * MIX_HALF)
        yield
        for j, name in ((1, "rk0"), (1, "rk1"), (0, "rr0"), (0, "rr1"), (2, "rv0"), (2, "rv1")):
            project(name, j * MIX_HALF + int(name[-1]) * half_w)
            yield
        og = FOX_Q_COL + 3 * MIX_HALF
        og_ref[:, :half_w] = _dot(h, w_ref[:, og:og + half_w]).astype(BF16)
        yield
        og_ref[:, half_w:] = _dot(h, w_ref[:, og + half_w:og + MIX_HALF]).astype(BF16)

    lane = _iota((1, LANES), 1)
    ones_h = _head_ones()

    def shifted(name, lo):
        p = proj.pop(name)
        cols = slice(lo, lo + p.shape[1])
        prev = jnp.where(_iota(p.shape, 0) == 0, prev_ref[7:8, cols], pltpu.roll(p, 1, axis=0))
        prev_ref[:, cols] = p[tm - 8:, :]
        return p + (prev - p) * mu_ref[:, cols]

    def prepare():
        logf = -_softplus(-(proj.pop("fl") + fb_ref[...]))
        blk = min(tm, CUMSUM_ROWS)
        tri = jnp.where(_iota((blk, blk), 1) <= _iota((blk, blk), 0), 1.0, 0.0).astype(BF16)
        run = carry_ref[0:1, :]
        c_blocks = []
        for r0 in range(0, tm, blk):
            cb = _dot_exact_lhs(tri, logf[r0:r0 + blk, :]) + run
            run = cb[blk - 1:blk, :]
            c_blocks.append(cb)
        carry_ref[...] = jnp.broadcast_to(run, carry_ref.shape)
        c = jnp.concatenate(c_blocks, axis=0)
        c_hi, c_mid, c_lo = (p.astype(F32) for p in _split3(c * LOG2E))
        packed = jnp.where(lane < N_HEADS, c_hi, jnp.where(lane < 2 * N_HEADS, pltpu.roll(c_mid, N_HEADS, axis=1),
                                                           pltpu.roll(c_lo, 2 * N_HEADS, axis=1)))
        aug = _dot(packed.astype(BF16), sel_ref[...])
        r2 = _iota((2 * LANES, 2 * LANES), 0) // HEAD_DIM
        c2 = _iota((2 * LANES, 2 * LANES), 1) // HEAD_DIM
        ones_qk = jnp.where(r2 == c2, 1.0, 0.0).astype(BF16)
        yield
        for pb in range(PAIRS):
            sl = slice((pb % 2) * LANES, (pb % 2 + 1) * LANES)
            q, k, v = (proj[name + str(pb // 2)][:, sl] for name in ("fq", "fk", "fv"))
            ss = _dot(jnp.concatenate([q * q, k * k], axis=1).astype(BF16), ones_qk) * (1.0 / HEAD_DIM)
            qn = q * lax.rsqrt(ss[:, :LANES] + NORM_EPS) * (qw_ref[...] * (HEAD_DIM ** -0.5 * LOG2E))
            kn = k * lax.rsqrt(ss[:, LANES:] + NORM_EPS) * kw_ref[...]
            for half in range(2):
                hd = 2 * pb + half
                own = (lane // HEAD_DIM) == half
                a0 = HEAD_DIM * (1 - half)
                z = aug[:, hd * LANES:(hd + 1) * LANES]
                c_q = (lane >= a0) & (lane < a0 + 3)
                c_k = (lane >= a0 + 3) & (lane < a0 + 6)
                qa = jnp.where(own, qn, jnp.where(c_q, z, jnp.where(c_k, 1.0, 0.0)))
                ka = jnp.where(own, kn, jnp.where(c_k, z, jnp.where(c_q, 1.0, 0.0)))
                va = jnp.where(own, v, jnp.where(lane == a0, 1.0, 0.0))
                qa_ref[hd] = qa.astype(BF16)
                ka_ref[hd] = ka.astype(BF16)
                va_ref[hd] = va.astype(BF16)
            yield

        wg = shifted("wg", 3 * MIX_HALF)
        tanh_wa = jnp.tanh(wg[:, :LANES]).astype(BF16)
        wa = wg[:, :LANES].astype(BF16)
        sig_gl = _sigmoid(wg[:, LANES:]).astype(BF16)
        tri = _block_tri(blk)
        st = []
        for j in range(2):
            cj = slice(j * half_w, (j + 1) * half_w)
            w = -_softplus(-(w0_ref[:, cj] + _dot(tanh_wa, w2_ref[:, cj]))) - 0.5
            logd = -jnp.exp(w)
            a_sig = _sigmoid(a0_ref[:, cj] + _dot(wa, a2_ref[:, cj]))
            bg_ref[:, MIX_HALF + j * half_w:MIX_HALF + (j + 1) * half_w] = _dot(sig_gl, g2_ref[:, cj]).astype(BF16)
            cs = jnp.concatenate([_dot_exact_lhs(tri, logd[r0:r0 + blk, :]) for r0 in range(0, tm, blk)], axis=0)
            w_incl = jnp.exp(cs)
            wc_ref[:, :, cj] = w_incl.reshape(tm // CHUNK, CHUNK, half_w)[:, CHUNK - 1:, :]
            st.append(dict(a_sig=a_sig, w_incl=w_incl, w_inv=jnp.exp(-cs), w_excl=jnp.exp(cs - logd)))
            yield
        group = lambda g, j: slice(g * MIX_HALF + j * half_w, g * MIX_HALF + (j + 1) * half_w)
        for j in range(2):
            k = shifted("rk%d" % j, MIX_HALF + j * half_w)
            kk = k * kkw_ref[:, group(0, j)]
            kk = kk * lax.rsqrt(jnp.maximum(_head_sum(kk * kk, ones_h), 1e-24))
            st[j]["k2"] = k * (1.0 + (st[j]["a_sig"] - 1.0) * kaw_ref[:, group(0, j)])
            ops_ref[:, group(0, j)] = ((-kk) * st[j]["w_excl"]).astype(BF16)
            ops_ref[:, group(2, j)] = (kk * st[j]["a_sig"] * st[j]["w_inv"]).astype(BF16)
            ops_ref[:, group(3, j)] = (st[j]["k2"] * st[j]["w_inv"]).astype(BF16)
            yield
        for j in range(2):
            st[j]["r"] = shifted("rr%d" % j, j * half_w)
            ops_ref[:, group(1, j)] = (st[j]["r"] * st[j]["w_incl"]).astype(BF16)
            yield
        for j in range(2):
            v = shifted("rv%d" % j, 2 * MIX_HALF + j * half_w)
            ops_ref[:, group(4, j)] = v.astype(BF16)
            bonus = _head_sum(st[j]["r"] * st[j]["k2"] * rkw_ref[:, group(0, j)], ones_h) * v
            bg_ref[:, group(0, j)] = bonus.astype(BF16)
            yield

    streams = [matmuls(), prepare()]
    while streams:
        for g in list(streams):
            if next(g, streams) is streams:
                streams.remove(g)


def _inproj(x3, norm_w, w_packed, f_bias, qw, kw, rwkv_consts, tm):
    b, t, _ = x3.shape
    row = lambda w: pl.BlockSpec((None, tm, w), lambda bi, ti: (bi, ti, 0))
    head_spec = pl.BlockSpec((None, N_HEADS, tm, LANES), lambda bi, ti: (bi, 0, ti, 0))
    head_shape = jax.ShapeDtypeStruct((b, N_HEADS, t, LANES), BF16)
    consts = (norm_w, w_packed, f_bias, qw, kw, _aug_tables()) + tuple(rwkv_consts)
    return pl.pallas_call(
        functools.partial(_inproj_kernel, tm=tm),
        grid=(b, t // tm),
        in_specs=[row(D_MODEL)] + [_resident(a.shape) for a in consts],
        out_specs=[row(5 * MIX_HALF), row(2 * MIX_HALF),
                   pl.BlockSpec((None, tm // CHUNK, 1, MIX_HALF), lambda bi, ti: (bi, ti, 0, 0)),
                   row(MIX_HALF), head_spec, head_spec, head_spec],
        out_shape=[jax.ShapeDtypeStruct((b, t, 5 * MIX_HALF), BF16), jax.ShapeDtypeStruct((b, t, 2 * MIX_HALF), BF16),
                   jax.ShapeDtypeStruct((b, t // CHUNK, 1, MIX_HALF), F32),
                   jax.ShapeDtypeStruct((b, t, MIX_HALF), BF16), head_shape, head_shape, head_shape],
        scratch_shapes=[pltpu.VMEM((8, LANES), F32), pltpu.VMEM((8, RWKV_COLS), F32)],
        compiler_params=_params(("arbitrary", "arbitrary")),
        name="inproj_foxprep",
    )(x3, *consts)


def _rwkv_kernel(a_ref, r_ref, b_ref, k_ref, v_ref, bonus_ref, gate_ref, wc_ref, lnw_ref, lnb_ref,
                 y_ref, s_ref, *, tt):
    @pl.when(pl.program_id(1) == 0)
    def _():
        s_ref[...] = jnp.zeros_like(s_ref)

    nc = tt // CHUNK
    ones_h = _head_ones()
    m0 = _iota((1, 1, LANES), 2) < HEAD_DIM

    def units(x_ref):
        out = []
        for p in range(PAIRS):
            xp = x_ref[:, p * LANES:(p + 1) * LANES].reshape(nc, CHUNK, LANES)
            zero = jnp.zeros_like(xp)
            out.append(jnp.concatenate([jnp.where(m0, xp, zero), jnp.where(m0, zero, xp)], axis=1))
        return jnp.concatenate(out, axis=0)

    a_st, r_st, b_st, k_st, v_st = (units(ref) for ref in (a_ref, r_ref, b_ref, k_ref, v_ref))
    wc = jnp.concatenate([wc_ref[:, :, p * LANES:(p + 1) * LANES] for p in range(PAIRS)], axis=0)

    ri = _iota((1, LANES, LANES), 1)
    ci = _iota((1, LANES, LANES), 2)
    same = (ri // CHUNK) == (ci // CHUNK)
    strict = same & (ci < ri)
    incl = same & (ci <= ri)
    eye = jnp.where(ri == ci, 1.0, 0.0).astype(F32)

    bk = jnp.concatenate([b_st, k_st], axis=1)
    sc = _bmm(jnp.concatenate([a_st, r_st], axis=1), bk, _BNT)
    l_ab = jnp.where(strict, sc[:, :LANES, :LANES], 0.0)
    l_ak = jnp.where(strict, sc[:, :LANES, LANES:], 0.0)
    m_rbk = jnp.concatenate([jnp.where(incl, sc[:, LANES:, :LANES], 0.0),
                             jnp.where(incl, sc[:, LANES:, LANES:], 0.0)], axis=2).astype(BF16)

    t_inv = eye + l_ab
    lp = l_ab.astype(BF16)
    lp = _bmm(lp, lp).astype(BF16)
    for _ in range(CHUNK.bit_length() - 3):
        both = _bmm(jnp.concatenate([lp, t_inv.astype(BF16)], axis=1), lp)
        t_inv = t_inv + both[:, LANES:]
        lp = both[:, :LANES].astype(BF16)
    t_inv = t_inv + _bmm(t_inv, lp)

    lakv = _bmm(l_ak, v_st)
    au = _bmm(t_inv, jnp.concatenate([a_st, lakv.astype(BF16)], axis=2))
    au_b = au.astype(BF16)
    zv = jnp.concatenate([jnp.zeros_like(v_st), v_st], axis=2)
    ry = _bmm(m_rbk, jnp.concatenate([au_b, zv], axis=1))
    r2_st = r_st.astype(F32) + ry[:, :, :LANES]
    y0_st = ry[:, :, LANES:]
    g_mat = (eye + _bmm(au_b[:, :, :LANES], b_st, _BTN)) * wc
    h_mat = _bmm(jnp.concatenate([au_b[:, :, LANES:], v_st], axis=1), bk, _BTN) * wc

    def chunk(x, c):
        return x.reshape((PAIRS, nc) + x.shape[1:])[:, c]

    s = s_ref[...]
    ys = []
    for c in range(nc):
        y_st = _bmm(chunk(r2_st, c), s, _BNT) + chunk(y0_st, c)
        s = _bmm(s, chunk(g_mat, c)) + chunk(h_mat, c)
        ys.append(y_st[:, :CHUNK] + y_st[:, CHUNK:])
    s_ref[...] = s
    y = jnp.concatenate([jnp.concatenate([yc[p] for yc in ys], axis=0) for p in range(PAIRS)], axis=1)

    mean = _head_sum(y, ones_h) * (1.0 / HEAD_DIM)
    yc = y - mean
    var = _head_sum(yc * yc, ones_h) * (1.0 / HEAD_DIM)
    yn = yc * lax.rsqrt(var + LNX_EPS) * lnw_ref[...] + lnb_ref[...]
    y_ref[...] = ((yn + bonus_ref[...].astype(F32)) * gate_ref[...].astype(F32)).astype(y_ref.dtype)


def _rwkv(ops, bg, wc, lnw, lnb, tt):
    b, t, _ = ops.shape
    col = lambda cb: pl.BlockSpec((None, tt, MIX_HALF), lambda bi, ti, cb=cb: (bi, ti, cb))
    vec = pl.BlockSpec((1, MIX_HALF), lambda bi, ti: (0, 0))
    return pl.pallas_call(
        functools.partial(_rwkv_kernel, tt=tt),
        grid=(b, t // tt),
        in_specs=[col(i) for i in range(5)] + [col(0), col(1)]
        + [pl.BlockSpec((None, tt // CHUNK, 1, MIX_HALF), lambda bi, ti: (bi, ti, 0, 0)), vec, vec],
        out_specs=pl.BlockSpec((None, tt, MIX_HALF), lambda bi, ti: (bi, ti, 0)),
        out_shape=jax.ShapeDtypeStruct((b, t, MIX_HALF), BF16),
        scratch_shapes=[pltpu.VMEM((PAIRS, LANES, LANES), F32)],
        compiler_params=_params(("arbitrary", "arbitrary")),
        name="rwkv7",
    )(*([ops] * 5), bg, bg, wc, lnw, lnb)


def _fox_attn_kernel(qa_ref, ka_ref, va_ref, og_ref, ow_ref, y_ref, m_ref, acc_ref, done_ref, *, tq):
    lane = _iota((1, LANES), 1)
    ones_h = _head_ones()

    def finish(tile):
        t0 = pl.multiple_of(tile * tq, tq)
        halves = []
        for h in range(2):
            acc = done_ref[h]
            denom = jnp.sum(jnp.where(lane == HEAD_DIM * (1 - h), acc, 0.0), axis=1, keepdims=True)
            halves.append(acc / denom)
        o = jnp.where(lane < HEAD_DIM, halves[0], halves[1])
        ms = _head_sum(o * o, ones_h) * (1.0 / HEAD_DIM)
        gate = _sigmoid(og_ref[pl.ds(t0, tq), :].astype(F32))
        y_ref[pl.ds(t0, tq), :] = (o * lax.rsqrt(ms + NORM_EPS) * ow_ref[...] * gate).astype(y_ref.dtype)

    def q_tile(i, carry):
        q0 = pl.multiple_of(i * tq, tq)
        q = [qa_ref[h, pl.ds(q0, tq), :] for h in range(2)]

        rows = tq // FOX_DIAG_SPLITS
        parts = [(h, r * rows, (r + 1) * rows) for r in range(FOX_DIAG_SPLITS) for h in range(2)]
        s = [_dot(q[h][r0:r0 + rows, :], ka_ref[h, pl.ds(q0, nk), :], _NT) for h, r0, nk in parts]
        finish(jnp.maximum(i - 1, 0))
        s = [jnp.where(_iota((rows, nk), 1) <= _iota((rows, nk), 0) + r0, x, -jnp.inf)
             for x, (h, r0, nk) in zip(s, parts)]
        m_new = [jnp.broadcast_to(jnp.max(x, axis=1, keepdims=True), (rows, LANES)) for x in s]
        p = [jnp.concatenate([jnp.exp2(x[:, c:c + LANES] - m) for c in range(0, x.shape[1], LANES)],
                             axis=1).astype(BF16) for x, m in zip(s, m_new)]
        pv = [_dot(x, va_ref[h, pl.ds(q0, nk), :]) for x, (h, r0, nk) in zip(p, parts)]
        for x, m, (h, r0, nk) in zip(pv, m_new, parts):
            acc_ref[h, r0:r0 + rows, :] = x
            m_ref[h, r0:r0 + rows, :] = m

        def step(start, nk):
            s = [_dot(q[h], ka_ref[h, pl.ds(start, nk), :], _NT) for h in range(2)]
            m_old = [m_ref[h] for h in range(2)]
            m_new = [jnp.maximum(m_old[h], jnp.max(s[h], axis=1, keepdims=True)) for h in range(2)]
            p = [jnp.concatenate([jnp.exp2(s[h][:, c:c + LANES] - m_new[h]) for c in range(0, nk, LANES)],
                                 axis=1).astype(BF16) for h in range(2)]
            pv = [_dot(p[h], va_ref[h, pl.ds(start, nk), :]) for h in range(2)]
            for h in range(2):
                acc_ref[h] = jnp.exp2(m_old[h] - m_new[h]) * acc_ref[h] + pv[h]
                m_ref[h] = m_new[h]

        width = FOX_KEY_TILES
        n_big = i // width

        def big_step(jj, c):
            step(pl.multiple_of(jj * (width * tq), width * tq), width * tq)
            return c

        lax.fori_loop(0, n_big, big_step, 0)
        rest = i - n_big * width
        done = n_big * width
        while width > 1:
            width //= 2

            @pl.when(rest % (2 * width) >= width)
            def _(width=width, done=done, rest=rest):
                tiles_before = done + (rest // (2 * width)) * (2 * width)
                step(pl.multiple_of(tiles_before * tq, width * tq), width * tq)

        done_ref[...] = acc_ref[...]
        return carry

    nq = qa_ref.shape[1] // tq
    done_ref[...] = jnp.ones_like(done_ref)
    lax.fori_loop(0, nq, q_tile, 0)
    finish(nq - 1)


def _fox_attn(qa, ka, va, og, ow, tq):
    b, _, t, _ = qa.shape
    seq_spec = pl.BlockSpec((None, 2, t, LANES), lambda bi, hp: (bi, hp, 0, 0))
    col_spec = pl.BlockSpec((None, t, LANES), lambda bi, hp: (bi, 0, hp))
    return pl.pallas_call(
        functools.partial(_fox_attn_kernel, tq=tq),
        grid=(b, PAIRS),
        in_specs=[seq_spec, seq_spec, seq_spec, col_spec, pl.BlockSpec((1, LANES), lambda bi, hp: (0, 0))],
        out_specs=col_spec,
        out_shape=jax.ShapeDtypeStruct((b, t, MIX_HALF), BF16),
        scratch_shapes=[pltpu.VMEM((2, tq, LANES), F32)] * 3,
        compiler_params=_params(("arbitrary", "arbitrary")),
        name="fox_attn",
    )(qa, ka, va, og, ow)


def _ffn_kernel(x_ref, yr_ref, yf_ref, wo_ref, nf_ref, wu_ref, cw_ref, cb_ref, wd_ref, nfin_ref, o_ref,
                x2_ref, hid_ref, carry_ref, *, tm):
    @pl.when(pl.program_id(1) == 0)
    def _():
        carry_ref[...] = jnp.zeros_like(carry_ref)

    x2 = (x_ref[...] + _dot(yr_ref[...], wo_ref[:MIX_HALF, :]) + _dot(yf_ref[...], wo_ref[MIX_HALF:, :]))
    x2_ref[...] = x2
    ms = jnp.mean(x2 * x2, axis=-1, keepdims=True)
    h2 = (x2 * lax.rsqrt(ms + NORM_EPS) * nf_ref[...]).astype(BF16)

    def up(lo, w):
        return [_dot(h2, wu_ref[:, off + lo:off + lo + w]) for off in (0, D_FF)]

    def conv(u, cols):
        ext = jnp.concatenate([carry_ref[:, cols], u], axis=0)
        carry_ref[:, cols] = u[tm - 8:, :]
        u1 = pltpu.roll(ext, 1, axis=0)[8:, :]
        u2 = pltpu.roll(ext, 2, axis=0)[8:, :]
        cw = cw_ref[:, cols]
        return cw[0:1, :] * u2 + cw[1:2, :] * u1 + cw[2:3, :] * u + cb_ref[:, cols]

    def conv_glu(lo, w, u):
        gate = conv(u[0], slice(lo, lo + w))
        val = conv(u[1], slice(D_FF + lo, D_FF + lo + w))
        hid_ref[:, lo:lo + w] = (gate * _sigmoid(gate) * val).astype(BF16)

    chunks = [(lo, min(FF_CHUNK, D_FF - lo)) for lo in range(0, D_FF, FF_CHUNK)]
    u_prev = up(*chunks[0])
    for prev, cur in zip(chunks[:-1], chunks[1:]):
        u_next = up(*cur)
        conv_glu(*prev, u_prev)
        u_prev = u_next
    conv_glu(*chunks[-1], u_prev)

    xo = x2_ref[...] + _dot(hid_ref[...], wd_ref[...])
    ms = jnp.mean(xo * xo, axis=-1, keepdims=True)
    o_ref[...] = xo * lax.rsqrt(ms + NORM_EPS) * nfin_ref[...]


def _ffn(x3, yr, yf, wo, nf, wu, cw, cb, wd, nfin, tm):
    b, t, _ = x3.shape
    row = lambda w: pl.BlockSpec((None, tm, w), lambda bi, ti: (bi, ti, 0))
    return pl.pallas_call(
        functools.partial(_ffn_kernel, tm=tm),
        grid=(b, t // tm),
        in_specs=[row(D_MODEL), row(MIX_HALF), row(MIX_HALF)]
        + [_resident(a.shape) for a in (wo, nf, wu, cw, cb, wd, nfin)],
        out_specs=row(D_MODEL),
        out_shape=jax.ShapeDtypeStruct(x3.shape, F32),
        scratch_shapes=[pltpu.VMEM((tm, D_MODEL), F32), pltpu.VMEM((tm, D_FF), BF16),
                        pltpu.VMEM((8, 2 * D_FF), F32)],
        compiler_params=_params(("arbitrary", "arbitrary")),
        name="outproj_convffn",
    )(x3, yr, yf, wo, nf, wu, cw, cb, wd, nfin)


def _pack_w_in(w_in):
    fox = w_in[:, RWKV_COLS:]
    fl = fox[:, 4 * MIX_HALF:]
    pad = jnp.zeros((D_MODEL, FOX_Q_COL - RWKV_COLS - fl.shape[1]), w_in.dtype)
    return jnp.concatenate([w_in[:, :RWKV_COLS], fl, pad, fox[:, :4 * MIX_HALF]], axis=1).astype(BF16)


def _tile(n, pref):
    t = min(n, pref)
    assert n % t == 0, (n, t)
    return t


def kernel(x, norm_mix_w, w_in, rwkv_mu, rwkv_w0, rwkv_w2, rwkv_a0, rwkv_a2, rwkv_g2, rwkv_k_k, rwkv_k_a,
           rwkv_r_k, rwkv_lnx_w, rwkv_lnx_b, fox_f_bias, fox_q_norm_w, fox_k_norm_w, fox_o_norm_w, w_out,
           norm_ffn_w, ffn_w_up, ffn_conv_w, ffn_conv_b, ffn_w_down, norm_final_w):
    b, t, d = x.shape
    assert d == D_MODEL and norm_mix_w.shape[0] == 1 and t % CHUNK == 0
    row = lambda a: a.reshape(1, -1).astype(F32)
    twice = lambda a: jnp.tile(a.reshape(1, HEAD_DIM), (1, 2)).astype(F32)

    fb = jnp.zeros((1, LANES), F32).at[0, :N_HEADS].set(fox_f_bias[0])
    zeros = jnp.zeros((DECAY_LORA, MIX_HALF), F32)
    w2p = jnp.concatenate([rwkv_w2[0], zeros], axis=0).astype(BF16)
    a2p = jnp.concatenate([zeros, rwkv_a2[0]], axis=0).astype(BF16)
    rwkv_consts = [row(a[0]) for a in (rwkv_mu, rwkv_w0, rwkv_a0, rwkv_k_k, rwkv_k_a, rwkv_r_k)]
    rwkv_consts += [w2p, a2p, rwkv_g2[0].astype(BF16)]
    ops, bg, wc, og, qa, ka, va = _inproj(x, row(norm_mix_w[0]), _pack_w_in(w_in[0]), fb, twice(fox_q_norm_w[0]),
                                          twice(fox_k_norm_w[0]), rwkv_consts, _tile(t, TILE_INPROJ))

    y_rwkv = _rwkv(ops, bg, wc, row(rwkv_lnx_w[0]), row(rwkv_lnx_b[0]), _tile(t, TILE_RWKV))

    y_fox = _fox_attn(qa, ka, va, og, twice(fox_o_norm_w[0]), _tile(t, TILE_FOX_ATTN))

    out = _ffn(x, y_rwkv, y_fox, w_out[0].astype(BF16), row(norm_ffn_w[0]),
               ffn_w_up[0].astype(BF16), ffn_conv_w[0].astype(F32), row(ffn_conv_b[0]),
               ffn_w_down[0].astype(BF16), row(norm_final_w),
               _tile(t, TILE_FFN))
    return out.astype(x.dtype)
```

```python
import functools

import jax
import jax.numpy as jnp
import numpy as np
from jax import lax
from jax.experimental import pallas as pl
from jax.experimental.pallas import tpu as pltpu

F32 = jnp.float32
BF16 = jnp.bfloat16

D_MODEL = 1024
HEAD_DIM = 64
N_HEADS = 8
MIX_HALF = N_HEADS * HEAD_DIM
DECAY_LORA = 64
AAA_LORA = 64
GATE_LORA = 128
RWKV_COLS = 3 * MIX_HALF + DECAY_LORA + AAA_LORA + GATE_LORA
D_FF = 2816
NORM_EPS = 1e-6
LNX_EPS = 64e-5

LANES = 128
PAIRS = MIX_HALF // LANES
CHUNK = 64
FL_BLOCK = RWKV_COLS // LANES
FOX_Q_COL = 2048
FF_CHUNK = 512
CUMSUM_ROWS = 256
VMEM_LIMIT = 56 * 1024 * 1024
LOG2E = 1.4426950408889634

TILE_INPROJ = 512
TILE_RWKV = 512
TILE_FOX_ATTN = 1024
FOX_KEY_TILES = 2
FOX_DIAG_SPLITS = 4
TILE_FFN = 512


def _params(sem):
    return pltpu.CompilerParams(dimension_semantics=sem, vmem_limit_bytes=VMEM_LIMIT)


def _split3(x):
    hi = x.astype(BF16)
    r1 = x - hi.astype(F32)
    mid = r1.astype(BF16)
    lo = (r1 - mid.astype(F32)).astype(BF16)
    return hi, mid, lo


def _dot(a, b, dims=None):
    if dims is None:
        dims = (((a.ndim - 1,), (0,)), ((), ()))
    return lax.dot_general(a, b, dims, preferred_element_type=F32)


_NT = (((1,), (1,)), ((), ()))

_BNN = (((2,), (1,)), ((0,), (0,)))
_BNT = (((2,), (2,)), ((0,), (0,)))
_BTN = (((1,), (1,)), ((0,), (0,)))


def _bmm(a, b, dims=_BNN):
    return lax.dot_general(a.astype(BF16), b.astype(BF16), dims, preferred_element_type=F32)


def _dot_exact_lhs(lhs_bf16, x):
    hi, mid, lo = _split3(x)
    return _dot(lhs_bf16, hi) + _dot(lhs_bf16, mid) + _dot(lhs_bf16, lo)


def _iota(shape, dim):
    return lax.broadcasted_iota(jnp.int32, shape, dim)


def _head_ones():
    r = _iota((LANES, LANES), 0) // HEAD_DIM
    c = _iota((LANES, LANES), 1) // HEAD_DIM
    return jnp.where(r == c, 1.0, 0.0).astype(BF16)


def _head_sum(x, ones_h):
    xb = x.astype(BF16)
    return jnp.concatenate([_dot(xb[:, i:i + LANES], ones_h) for i in range(0, x.shape[1], LANES)], axis=1)


def _block_tri(n):
    r = _iota((n, n), 0)
    c = _iota((n, n), 1)
    return jnp.where((r // CHUNK == c // CHUNK) & (c <= r), 1.0, 0.0).astype(BF16)


def _sigmoid(z):
    return 1.0 / (1.0 + jnp.exp(-z))


def _softplus(z):
    return jnp.maximum(z, 0.0) + jnp.log1p(jnp.exp(-jnp.abs(z)))


def _resident(shape):
    return pl.BlockSpec(shape, lambda *_: (0,) * len(shape), pipeline_mode=pl.Buffered(1))


def _aug_tables():
    sel = np.zeros((LANES, N_HEADS * LANES), np.float32)
    for h in range(N_HEADS):
        a0 = h * LANES + HEAD_DIM * (1 - h % 2)
        for i in range(3):
            sel[i * N_HEADS + h, a0 + i] = 1.0
            sel[i * N_HEADS + h, a0 + 3 + i] = -1.0
    return jnp.asarray(sel, BF16)


def _inproj_kernel(x_ref, nw_ref, w_ref, fb_ref, qw_ref, kw_ref, sel_ref,
                   mu_ref, w0_ref, a0_ref, kkw_ref, kaw_ref, rkw_ref, w2_ref, a2_ref, g2_ref,
                   ops_ref, bg_ref, wc_ref, og_ref, qa_ref, ka_ref, va_ref, carry_ref, prev_ref, *, tm):
    @pl.when(pl.program_id(1) == 0)
    def _():
        carry_ref[...] = jnp.zeros_like(carry_ref)
        prev_ref[...] = jnp.zeros_like(prev_ref)

    x = x_ref[...]
    ms = jnp.mean(x * x, axis=-1, keepdims=True)
    h = (x * lax.rsqrt(ms + NORM_EPS) * nw_ref[...]).astype(BF16)
    half_w = MIX_HALF // 2
    proj = {}

    def project(name, lo, width=half_w):
        proj[name] = _dot(h, w_ref[:, lo:lo + width])

    def matmuls():
        project("fl", FL_BLOCK * LANES, LANES)
        for j, name in enumerate(("fq0", "fk0", "fv0")):
            project(name, FOX_Q_COL + j * MIX_HALF)
        yield
        for j, name in enumerate(("fq1", "fk1", "fv1")):
            project(name, FOX_Q_COL + j * MIX_HALF + half_w)
            yield
        project("wg", 3 * MIX_HALF)
        yield
        for j, name in ((1, "rk0"), (1, "rk1"), (0, "rr0"), (0, "rr1"), (2, "rv0"), (2, "rv1")):
            project(name, j * MIX_HALF + int(name[-1]) * half_w)
            yield
        og = FOX_Q_COL + 3 * MIX_HALF
        og_ref[:, :half_w] = _dot(h, w_ref[:, og:og + half_w]).astype(BF16)
        yield
        og_ref[:, half_w:] = _dot(h, w_ref[:, og + half_w:og + MIX_HALF]).astype(BF16)

    lane = _iota((1, LANES), 1)
    ones_h = _head_ones()

    def shifted(name, lo):
        p = proj.pop(name)
        cols = slice(lo, lo + p.shape[1])
        prev = jnp.where(_iota(p.shape, 0) == 0, prev_ref[7:8, cols], pltpu.roll(p, 1, axis=0))
        prev_ref[:, cols] = p[tm - 8:, :]
        return p + (prev - p) * mu_ref[:, cols]

    def prepare():
        logf = -_softplus(-(proj.pop("fl") + fb_ref[...]))
        blk = min(tm, CUMSUM_ROWS)
        tri = jnp.where(_iota((blk, blk), 1) <= _iota((blk, blk), 0), 1.0, 0.0).astype(BF16)
        run = carry_ref[0:1, :]
        c_blocks = []
        for r0 in range(0, tm, blk):
            cb = _dot_exact_lhs(tri, logf[r0:r0 + blk, :]) + run
            run = cb[blk - 1:blk, :]
            c_blocks.append(cb)
        carry_ref[...] = jnp.broadcast_to(run, carry_ref.shape)
        c = jnp.concatenate(c_blocks, axis=0)
        c_hi, c_mid, c_lo = (p.astype(F32) for p in _split3(c * LOG2E))
        packed = jnp.where(lane < N_HEADS, c_hi, jnp.where(lane < 2 * N_HEADS, pltpu.roll(c_mid, N_HEADS, axis=1),
                                                           pltpu.roll(c_lo, 2 * N_HEADS, axis=1)))
        aug = _dot(packed.astype(BF16), sel_ref[...])
        r2 = _iota((2 * LANES, 2 * LANES), 0) // HEAD_DIM
        c2 = _iota((2 * LANES, 2 * LANES), 1) // HEAD_DIM
        ones_qk = jnp.where(r2 == c2, 1.0, 0.0).astype(BF16)
        yield
        for pb in range(PAIRS):
            sl = slice((pb % 2) * LANES, (pb % 2 + 1) * LANES)
            q, k, v = (proj[name + str(pb // 2)][:, sl] for name in ("fq", "fk", "fv"))
            ss = _dot(jnp.concatenate([q * q, k * k], axis=1).astype(BF16), ones_qk) * (1.0 / HEAD_DIM)
            qn = q * lax.rsqrt(ss[:, :LANES] + NORM_EPS) * (qw_ref[...] * (HEAD_DIM ** -0.5 * LOG2E))
            kn = k * lax.rsqrt(ss[:, LANES:] + NORM_EPS) * kw_ref[...]
            for half in range(2):
                hd = 2 * pb + half
                own = (lane // HEAD_DIM) == half
                a0 = HEAD_DIM * (1 - half)
                z = aug[:, hd * LANES:(hd + 1) * LANES]
                c_q = (lane >= a0) & (lane < a0 + 3)
                c_k = (lane >= a0 + 3) & (lane < a0 + 6)
                qa = jnp.where(own, qn, jnp.where(c_q, z, jnp.where(c_k, 1.0, 0.0)))
                ka = jnp.where(own, kn, jnp.where(c_k, z, jnp.where(c_q, 1.0, 0.0)))
                va = jnp.where(own, v, jnp.where(lane == a0, 1.0, 0.0))
                qa_ref[hd] = qa.astype(BF16)
                ka_ref[hd] = ka.astype(BF16)
                va_ref[hd] = va.astype(BF16)
            yield

        wg = shifted("wg", 3 * MIX_HALF)
        tanh_wa = jnp.tanh(wg[:, :LANES]).astype(BF16)
        wa = wg[:, :LANES].astype(BF16)
        sig_gl = _sigmoid(wg[:, LANES:]).astype(BF16)
        tri = _block_tri(blk)
        st = []
        for j in range(2):
            cj = slice(j * half_w, (j + 1) * half_w)
            w = -_softplus(-(w0_ref[:, cj] + _dot(tanh_wa, w2_ref[:, cj]))) - 0.5
            logd = -jnp.exp(w)
            a_sig = _sigmoid(a0_ref[:, cj] + _dot(wa, a2_ref[:, cj]))
            bg_ref[:, MIX_HALF + j * half_w:MIX_HALF + (j + 1) * half_w] = _dot(sig_gl, g2_ref[:, cj]).astype(BF16)
            cs = jnp.concatenate([_dot_exact_lhs(tri, logd[r0:r0 + blk, :]) for r0 in range(0, tm, blk)], axis=0)
            w_incl = jnp.exp(cs)
            wc_ref[:, :, cj] = w_incl.reshape(tm // CHUNK, CHUNK, half_w)[:, CHUNK - 1:, :]
            st.append(dict(a_sig=a_sig, w_incl=w_incl, w_inv=jnp.exp(-cs), w_excl=jnp.exp(cs - logd)))
            yield
        group = lambda g, j: slice(g * MIX_HALF + j * half_w, g * MIX_HALF + (j + 1) * half_w)
        for j in range(2):
            k = shifted("rk%d" % j, MIX_HALF + j * half_w)
            kk = k * kkw_ref[:, group(0, j)]
            kk = kk * lax.rsqrt(jnp.maximum(_head_sum(kk * kk, ones_h), 1e-24))
            st[j]["k2"] = k * (1.0 + (st[j]["a_sig"] - 1.0) * kaw_ref[:, group(0, j)])
            ops_ref[:, group(0, j)] = ((-kk) * st[j]["w_excl"]).astype(BF16)
            ops_ref[:, group(2, j)] = (kk * st[j]["a_sig"] * st[j]["w_inv"]).astype(BF16)
            ops_ref[:, group(3, j)] = (st[j]["k2"] * st[j]["w_inv"]).astype(BF16)
            yield
        for j in range(2):
            st[j]["r"] = shifted("rr%d" % j, j * half_w)
            ops_ref[:, group(1, j)] = (st[j]["r"] * st[j]["w_incl"]).astype(BF16)
            yield
        for j in range(2):
            v = shifted("rv%d" % j, 2 * MIX_HALF + j * half_w)
            ops_ref[:, group(4, j)] = v.astype(BF16)
            bonus = _head_sum(st[j]["r"] * st[j]["k2"] * rkw_ref[:, group(0, j)], ones_h) * v
            bg_ref[:, group(0, j)] = bonus.astype(BF16)
            yield

    streams = [matmuls(), prepare()]
    while streams:
        for g in list(streams):
            if next(g, streams) is streams:
                streams.remove(g)


def _inproj(x3, norm_w, w_packed, f_bias, qw, kw, rwkv_consts, tm):
    b, t, _ = x3.shape
    row = lambda w: pl.BlockSpec((None, tm, w), lambda bi, ti: (bi, ti, 0))
    head_spec = pl.BlockSpec((None, N_HEADS, tm, LANES), lambda bi, ti: (bi, 0, ti, 0))
    head_shape = jax.ShapeDtypeStruct((b, N_HEADS, t, LANES), BF16)
    consts = (norm_w, w_packed, f_bias, qw, kw, _aug_tables()) + tuple(rwkv_consts)
    return pl.pallas_call(
        functools.partial(_inproj_kernel, tm=tm),
        grid=(b, t // tm),
        in_specs=[row(D_MODEL)] + [_resident(a.shape) for a in consts],
        out_specs=[row(5 * MIX_HALF), row(2 * MIX_HALF),
                   pl.BlockSpec((None, tm // CHUNK, 1, MIX_HALF), lambda bi, ti: (bi, ti, 0, 0)),
                   row(MIX_HALF), head_spec, head_spec, head_spec],
        out_shape=[jax.ShapeDtypeStruct((b, t, 5 * MIX_HALF), BF16), jax.ShapeDtypeStruct((b, t, 2 * MIX_HALF), BF16),
                   jax.ShapeDtypeStruct((b, t // CHUNK, 1, MIX_HALF), F32),
                   jax.ShapeDtypeStruct((b, t, MIX_HALF), BF16), head_shape, head_shape, head_shape],
        scratch_shapes=[pltpu.VMEM((8, LANES), F32), pltpu.VMEM((8, RWKV_COLS), F32)],
        compiler_params=_params(("arbitrary", "arbitrary")),
        name="inproj_foxprep",
    )(x3, *consts)


def _rwkv_kernel(a_ref, r_ref, b_ref, k_ref, v_ref, bonus_ref, gate_ref, wc_ref, lnw_ref, lnb_ref,
                 y_ref, s_ref, *, tt):
    @pl.when(pl.program_id(1) == 0)
    def _():
        s_ref[...] = jnp.zeros_like(s_ref)

    nc = tt // CHUNK
    ones_h = _head_ones()
    m0 = _iota((1, 1, LANES), 2) < HEAD_DIM

    ri = _iota((1, LANES, LANES), 1)
    ci = _iota((1, LANES, LANES), 2)
    same = (ri // CHUNK) == (ci // CHUNK)
    strict = same & (ci < ri)
    incl = same & (ci <= ri)
    eye = jnp.where(ri == ci, 1.0, 0.0).astype(F32)

    def stages(c0, c1, out):
        n = c1 - c0

        def units(x_ref):
            parts = []
            for p in range(PAIRS):
                xp = x_ref[c0 * CHUNK:c1 * CHUNK, p * LANES:(p + 1) * LANES].reshape(n, CHUNK, LANES)
                zero = jnp.zeros_like(xp)
                parts.append(jnp.concatenate([jnp.where(m0, xp, zero), jnp.where(m0, zero, xp)], axis=1))
            return jnp.concatenate(parts, axis=0)

        a_st, r_st, b_st, k_st, v_st = (units(ref) for ref in (a_ref, r_ref, b_ref, k_ref, v_ref))
        wc = jnp.concatenate([wc_ref[c0:c1, :, p * LANES:(p + 1) * LANES] for p in range(PAIRS)], axis=0)
        bk = jnp.concatenate([b_st, k_st], axis=1)
        sc = _bmm(jnp.concatenate([a_st, r_st], axis=1), bk, _BNT)
        yield
        l_ab = jnp.where(strict, sc[:, :LANES, :LANES], 0.0)
        l_ak = jnp.where(strict, sc[:, :LANES, LANES:], 0.0)
        m_rbk = jnp.concatenate([jnp.where(incl, sc[:, LANES:, :LANES], 0.0),
                                 jnp.where(incl, sc[:, LANES:, LANES:], 0.0)], axis=2).astype(BF16)
        t_inv = eye + l_ab
        lp = l_ab.astype(BF16)
        lp = _bmm(lp, lp).astype(BF16)
        yield
        for _ in range(CHUNK.bit_length() - 3):
            both = _bmm(jnp.concatenate([lp, t_inv.astype(BF16)], axis=1), lp)
            t_inv = t_inv + both[:, LANES:]
            lp = both[:, :LANES].astype(BF16)
            yield
        t_inv = t_inv + _bmm(t_inv, lp)
        lakv = _bmm(l_ak, v_st)
        yield
        au = _bmm(t_inv, jnp.concatenate([a_st, lakv.astype(BF16)], axis=2))
        au_b = au.astype(BF16)
        yield
        zv = jnp.concatenate([jnp.zeros_like(v_st), v_st], axis=2)
        ry = _bmm(m_rbk, jnp.concatenate([au_b, zv], axis=1))
        split = lambda x: x.reshape((PAIRS, n) + x.shape[1:])
        out["r2"] = split(r_st.astype(F32) + ry[:, :, :LANES])
        out["y0"] = split(ry[:, :, LANES:])
        yield
        out["g"] = split((eye + _bmm(au_b[:, :, :LANES], b_st, _BTN)) * wc)
        out["h"] = split(_bmm(jnp.concatenate([au_b[:, :, LANES:], v_st], axis=1), bk, _BTN) * wc)

    ys = []

    def chain(res):
        s = s_ref[...]
        for c in range(res["g"].shape[1]):
            y_st = _bmm(res["r2"][:, c], s, _BNT) + res["y0"][:, c]
            s = _bmm(s, res["g"][:, c]) + res["h"][:, c]
            ys.append(y_st[:, :CHUNK] + y_st[:, CHUNK:])
            s_ref[...] = s
            yield

    def run(*gens):
        gens = list(gens)
        while gens:
            for g in list(gens):
                if next(g, gens) is gens:
                    gens.remove(g)

    first, second = {}, {}
    run(stages(0, nc // 2, first))
    run(stages(nc // 2, nc, second), chain(first))
    run(chain(second))
    y = jnp.concatenate([jnp.concatenate([yc[p] for yc in ys], axis=0) for p in range(PAIRS)], axis=1)

    mean = _head_sum(y, ones_h) * (1.0 / HEAD_DIM)
    yc = y - mean
    var = _head_sum(yc * yc, ones_h) * (1.0 / HEAD_DIM)
    yn = yc * lax.rsqrt(var + LNX_EPS) * lnw_ref[...] + lnb_ref[...]
    y_ref[...] = ((yn + bonus_ref[...].astype(F32)) * gate_ref[...].astype(F32)).astype(y_ref.dtype)


def _rwkv(ops, bg, wc, lnw, lnb, tt):
    b, t, _ = ops.shape
    assert (tt // CHUNK) % 2 == 0, tt
    col = lambda cb: pl.BlockSpec((None, tt, MIX_HALF), lambda bi, ti, cb=cb: (bi, ti, cb))
    vec = pl.BlockSpec((1, MIX_HALF), lambda bi, ti: (0, 0))
    return pl.pallas_call(
        functools.partial(_rwkv_kernel, tt=tt),
        grid=(b, t // tt),
        in_specs=[col(i) for i in range(5)] + [col(0), col(1)]
        + [pl.BlockSpec((None, tt // CHUNK, 1, MIX_HALF), lambda bi, ti: (bi, ti, 0, 0)), vec, vec],
        out_specs=pl.BlockSpec((None, tt, MIX_HALF), lambda bi, ti: (bi, ti, 0)),
        out_shape=jax.ShapeDtypeStruct((b, t, MIX_HALF), BF16),
        scratch_shapes=[pltpu.VMEM((PAIRS, LANES, LANES), F32)],
        compiler_params=_params(("arbitrary", "arbitrary")),
        name="rwkv7",
    )(*([ops] * 5), bg, bg, wc, lnw, lnb)


def _fox_attn_kernel(qa_ref, ka_ref, va_ref, og_ref, ow_ref, y_ref, m_ref, acc_ref, done_ref, *, tq):
    lane = _iota((1, LANES), 1)
    ones_h = _head_ones()

    def finish(tile):
        t0 = pl.multiple_of(tile * tq, tq)
        halves = []
        for h in range(2):
            acc = done_ref[h]
            denom = jnp.sum(jnp.where(lane == HEAD_DIM * (1 - h), acc, 0.0), axis=1, keepdims=True)
            halves.append(acc / denom)
        o = jnp.where(lane < HEAD_DIM, halves[0], halves[1])
        ms = _head_sum(o * o, ones_h) * (1.0 / HEAD_DIM)
        gate = _sigmoid(og_ref[pl.ds(t0, tq), :].astype(F32))
        y_ref[pl.ds(t0, tq), :] = (o * lax.rsqrt(ms + NORM_EPS) * ow_ref[...] * gate).astype(y_ref.dtype)

    def q_tile(i, carry):
        q0 = pl.multiple_of(i * tq, tq)
        q = [qa_ref[h, pl.ds(q0, tq), :] for h in range(2)]

        rows = tq // FOX_DIAG_SPLITS
        parts = [(h, r * rows, (r + 1) * rows) for r in range(FOX_DIAG_SPLITS) for h in range(2)]
        s = [_dot(q[h][r0:r0 + rows, :], ka_ref[h, pl.ds(q0, nk), :], _NT) for h, r0, nk in parts]
        finish(jnp.maximum(i - 1, 0))
        s = [jnp.where(_iota((rows, nk), 1) <= _iota((rows, nk), 0) + r0, x, -jnp.inf)
             for x, (h, r0, nk) in zip(s, parts)]
        m_new = [jnp.broadcast_to(jnp.max(x, axis=1, keepdims=True), (rows, LANES)) for x in s]
        p = [jnp.concatenate([jnp.exp2(x[:, c:c + LANES] - m) for c in range(0, x.shape[1], LANES)],
                             axis=1).astype(BF16) for x, m in zip(s, m_new)]
        pv = [_dot(x, va_ref[h, pl.ds(q0, nk), :]) for x, (h, r0, nk) in zip(p, parts)]
        for x, m, (h, r0, nk) in zip(pv, m_new, parts):
            acc_ref[h, r0:r0 + rows, :] = x
            m_ref[h, r0:r0 + rows, :] = m

        def step(start, nk):
            s = [_dot(q[h], ka_ref[h, pl.ds(start, nk), :], _NT) for h in range(2)]
            m_old = [m_ref[h] for h in range(2)]
            m_new = [jnp.maximum(m_old[h], jnp.max(s[h], axis=1, keepdims=True)) for h in range(2)]
            p = [jnp.concatenate([jnp.exp2(s[h][:, c:c + LANES] - m_new[h]) for c in range(0, nk, LANES)],
                                 axis=1).astype(BF16) for h in range(2)]
            pv = [_dot(p[h], va_ref[h, pl.ds(start, nk), :]) for h in range(2)]
            for h in range(2):
                acc_ref[h] = jnp.exp2(m_old[h] - m_new[h]) * acc_ref[h] + pv[h]
                m_ref[h] = m_new[h]

        width = FOX_KEY_TILES
        n_big = i // width

        def big_step(jj, c):
            step(pl.multiple_of(jj * (width * tq), width * tq), width * tq)
            return c

        lax.fori_loop(0, n_big, big_step, 0)
        rest = i - n_big * width
        done = n_big * width
        while width > 1:
            width //= 2

            @pl.when(rest % (2 * width) >= width)
            def _(width=width, done=done, rest=rest):
                tiles_before = done + (rest // (2 * width)) * (2 * width)
                step(pl.multiple_of(tiles_before * tq, width * tq), width * tq)

        done_ref[...] = acc_ref[...]
        return carry

    nq = qa_ref.shape[1] // tq
    done_ref[...] = jnp.ones_like(done_ref)
    lax.fori_loop(0, nq, q_tile, 0)
    finish(nq - 1)


def _fox_attn(qa, ka, va, og, ow, tq):
    b, _, t, _ = qa.shape
    seq_spec = pl.BlockSpec((None, 2, t, LANES), lambda bi, hp: (bi, hp, 0, 0))
    col_spec = pl.BlockSpec((None, t, LANES), lambda bi, hp: (bi, 0, hp))
    return pl.pallas_call(
        functools.partial(_fox_attn_kernel, tq=tq),
        grid=(b, PAIRS),
        in_specs=[seq_spec, seq_spec, seq_spec, col_spec, pl.BlockSpec((1, LANES), lambda bi, hp: (0, 0))],
        out_specs=col_spec,
        out_shape=jax.ShapeDtypeStruct((b, t, MIX_HALF), BF16),
        scratch_shapes=[pltpu.VMEM((2, tq, LANES), F32)] * 3,
        compiler_params=_params(("arbitrary", "arbitrary")),
        name="fox_attn",
    )(qa, ka, va, og, ow)


def _ffn_kernel(x_ref, yr_ref, yf_ref, wo_ref, nf_ref, wu_ref, cw_ref, cb_ref, wd_ref, nfin_ref, o_ref,
                x2_ref, hid_ref, carry_ref, *, tm):
    @pl.when(pl.program_id(1) == 0)
    def _():
        carry_ref[...] = jnp.zeros_like(carry_ref)

    x2 = (x_ref[...] + _dot(yr_ref[...], wo_ref[:MIX_HALF, :]) + _dot(yf_ref[...], wo_ref[MIX_HALF:, :]))
    x2_ref[...] = x2
    ms = jnp.mean(x2 * x2, axis=-1, keepdims=True)
    h2 = (x2 * lax.rsqrt(ms + NORM_EPS) * nf_ref[...]).astype(BF16)

    def up(lo, w):
        return [_dot(h2, wu_ref[:, off + lo:off + lo + w]) for off in (0, D_FF)]

    def conv(u, cols):
        ext = jnp.concatenate([carry_ref[:, cols], u], axis=0)
        carry_ref[:, cols] = u[tm - 8:, :]
        u1 = pltpu.roll(ext, 1, axis=0)[8:, :]
        u2 = pltpu.roll(ext, 2, axis=0)[8:, :]
        cw = cw_ref[:, cols]
        return cw[0:1, :] * u2 + cw[1:2, :] * u1 + cw[2:3, :] * u + cb_ref[:, cols]

    def conv_glu(lo, w, u):
        gate = conv(u[0], slice(lo, lo + w))
        val = conv(u[1], slice(D_FF + lo, D_FF + lo + w))
        hid_ref[:, lo:lo + w] = (gate * _sigmoid(gate) * val).astype(BF16)

    chunks = [(lo, min(FF_CHUNK, D_FF - lo)) for lo in range(0, D_FF, FF_CHUNK)]
    u_prev = up(*chunks[0])
    for prev, cur in zip(chunks[:-1], chunks[1:]):
        u_next = up(*cur)
        conv_glu(*prev, u_prev)
        u_prev = u_next
    conv_glu(*chunks[-1], u_prev)

    xo = x2_ref[...] + _dot(hid_ref[...], wd_ref[...])
    ms = jnp.mean(xo * xo, axis=-1, keepdims=True)
    o_ref[...] = xo * lax.rsqrt(ms + NORM_EPS) * nfin_ref[...]


def _ffn(x3, yr, yf, wo, nf, wu, cw, cb, wd, nfin, tm):
    b, t, _ = x3.shape
    row = lambda w: pl.BlockSpec((None, tm, w), lambda bi, ti: (bi, ti, 0))
    return pl.pallas_call(
        functools.partial(_ffn_kernel, tm=tm),
        grid=(b, t // tm),
        in_specs=[row(D_MODEL), row(MIX_HALF), row(MIX_HALF)]
        + [_resident(a.shape) for a in (wo, nf, wu, cw, cb, wd, nfin)],
        out_specs=row(D_MODEL),
        out_shape=jax.ShapeDtypeStruct(x3.shape, F32),
        scratch_shapes=[pltpu.VMEM((tm, D_MODEL), F32), pltpu.VMEM((tm, D_FF), BF16),
                        pltpu.VMEM((8, 2 * D_FF), F32)],
        compiler_params=_params(("arbitrary", "arbitrary")),
        name="outproj_convffn",
    )(x3, yr, yf, wo, nf, wu, cw, cb, wd, nfin)


def _pack_w_in(w_in):
    fox = w_in[:, RWKV_COLS:]
    fl = fox[:, 4 * MIX_HALF:]
    pad = jnp.zeros((D_MODEL, FOX_Q_COL - RWKV_COLS - fl.shape[1]), w_in.dtype)
    return jnp.concatenate([w_in[:, :RWKV_COLS], fl, pad, fox[:, :4 * MIX_HALF]], axis=1).astype(BF16)


def _tile(n, pref):
    t = min(n, pref)
    assert n % t == 0, (n, t)
    return t


def kernel(x, norm_mix_w, w_in, rwkv_mu, rwkv_w0, rwkv_w2, rwkv_a0, rwkv_a2, rwkv_g2, rwkv_k_k, rwkv_k_a,
           rwkv_r_k, rwkv_lnx_w, rwkv_lnx_b, fox_f_bias, fox_q_norm_w, fox_k_norm_w, fox_o_norm_w, w_out,
           norm_ffn_w, ffn_w_up, ffn_conv_w, ffn_conv_b, ffn_w_down, norm_final_w):
    b, t, d = x.shape
    assert d == D_MODEL and norm_mix_w.shape[0] == 1 and t % CHUNK == 0
    row = lambda a: a.reshape(1, -1).astype(F32)
    twice = lambda a: jnp.tile(a.reshape(1, HEAD_DIM), (1, 2)).astype(F32)

    fb = jnp.zeros((1, LANES), F32).at[0, :N_HEADS].set(fox_f_bias[0])
    zeros = jnp.zeros((DECAY_LORA, MIX_HALF), F32)
    w2p = jnp.concatenate([rwkv_w2[0], zeros], axis=0).astype(BF16)
    a2p = jnp.concatenate([zeros, rwkv_a2[0]], axis=0).astype(BF16)
    rwkv_consts = [row(a[0]) for a in (rwkv_mu, rwkv_w0, rwkv_a0, rwkv_k_k, rwkv_k_a, rwkv_r_k)]
    rwkv_consts += [w2p, a2p, rwkv_g2[0].astype(BF16)]
    ops, bg, wc, og, qa, ka, va = _inproj(x, row(norm_mix_w[0]), _pack_w_in(w_in[0]), fb, twice(fox_q_norm_w[0]),
                                          twice(fox_k_norm_w[0]), rwkv_consts, _tile(t, TILE_INPROJ))

    y_rwkv = _rwkv(ops, bg, wc, row(rwkv_lnx_w[0]), row(rwkv_lnx_b[0]), _tile(t, TILE_RWKV))

    y_fox = _fox_attn(qa, ka, va, og, twice(fox_o_norm_w[0]), _tile(t, TILE_FOX_ATTN))

    out = _ffn(x, y_rwkv, y_fox, w_out[0].astype(BF16), row(norm_ffn_w[0]),
               ffn_w_up[0].astype(BF16), ffn_conv_w[0].astype(F32), row(ffn_conv_b[0]),
               ffn_w_down[0].astype(BF16), row(norm_final_w),
               _tile(t, TILE_FFN))
    return out.astype(x.dtype)
```
